```python
import math
import jax, jax.numpy as jnp
from jax import lax
import numpy as np

D_MODEL = 2048
BATCH = 8
SEQ = 2048
DEPTH = 1
DEC_BATCH = 128
DEC_SEQ = 1
PAST_LEN = 2048
PAGE_SIZE = 128

MIX_WIDTH = D_MODEL
ATT_WIDTH = MIX_WIDTH // 2
CONV_WIDTH = MIX_WIDTH - ATT_WIDTH
V_DIM = 128
N_ATT_HEADS = ATT_WIDTH // V_DIM
QK_HALF = V_DIM // 2
QK_DIM = 2 * QK_HALF
CONV_K = 3
N_GROUPS = 4
EXPERTS_PER_GROUP = 8
N_EXPERTS = N_GROUPS * EXPERTS_PER_GROUP
TOP_K = 2
D_EXPERT = D_MODEL // 4
MOE_BLOCK = 128
Q_BLOCK = 128
EPS = 1e-6
IN_COLS = 3 * ATT_WIDTH + 3 * CONV_WIDTH

kernel_name = "hybrid_diffattn_shortconv_hiermoe_step"


def rms_norm(x, g):
    xf = x.astype(jnp.float32)
    y = xf * lax.rsqrt(jnp.mean(xf * xf, axis=-1, keepdims=True) + EPS)
    return (y * g.astype(jnp.float32)).astype(x.dtype)


def alibi_slopes():
    return 2.0 ** (-8.0 * jnp.arange(1, N_ATT_HEADS + 1, dtype=jnp.float32) / N_ATT_HEADS)


def diff_lambda(lq1, lk1, lq2, lk2, lam_init):
    f = lambda a: a.astype(jnp.float32)
    return jnp.exp(jnp.sum(f(lq1) * f(lk1))) - jnp.exp(jnp.sum(f(lq2) * f(lk2))) + lam_init


def project(h, w_in, g_q, g_k):
    b, t, _ = h.shape
    p = h @ w_in
    a = ATT_WIDTH
    q, k, v, bg, cg, xc = jnp.split(p, [a, 2 * a, 3 * a, 3 * a + CONV_WIDTH, 3 * a + 2 * CONV_WIDTH], axis=-1)
    q = rms_norm(q.reshape(b, t, N_ATT_HEADS, 2, QK_HALF), g_q)
    k = rms_norm(k.reshape(b, t, N_ATT_HEADS, 2, QK_HALF), g_k)
    v = v.reshape(b, t, N_ATT_HEADS, V_DIM)
    return q, k, v, bg, cg, xc


def diff_scores(q, k, q_pos, k_pos):
    s = jnp.einsum("bqhcd,bkhcd->bhcqk", q, k, preferred_element_type=jnp.float32) * (QK_HALF ** -0.5)
    dist = q_pos[:, None] - k_pos[None, :]
    s = s - alibi_slopes()[:, None, None, None] * dist.astype(jnp.float32)
    return jnp.where(dist >= 0, s, -jnp.inf)


def diff_weights(s, lam):
    p = jax.nn.softmax(s, axis=-1)
    return p[:, :, 0] - lam * p[:, :, 1]


def prompt_attention(q, k, v, lam):
    b, s = q.shape[:2]
    nblk = s // Q_BLOCK
    qb = jnp.moveaxis(q.reshape(b, nblk, Q_BLOCK, N_ATT_HEADS, 2, QK_HALF), 1, 0)
    pos = jnp.arange(s, dtype=jnp.int32)
    pb = pos.reshape(nblk, Q_BLOCK)

    def block(args):
        q_blk, p_blk = args
        a = diff_weights(diff_scores(q_blk, k, p_blk, pos), lam).astype(v.dtype)
        return jnp.einsum("bhqk,bkhd->bqhd", a, v)

    o = lax.map(block, (qb, pb))
    return jnp.moveaxis(o, 0, 1).reshape(b, s, N_ATT_HEADS, V_DIM)


def sample_attention(q, k_new, v_new, k_pages, v_pages, lam):
    nb, t = q.shape[:2]
    past = k_pages.shape[1] * PAGE_SIZE
    k_past = k_pages.reshape(nb, past, N_ATT_HEADS, 2, QK_HALF)
    v_past = v_pages.reshape(nb, past, N_ATT_HEADS, V_DIM)
    q_pos = past + jnp.arange(t, dtype=jnp.int32)
    s = jnp.concatenate([diff_scores(q, k_past, q_pos, jnp.arange(past, dtype=jnp.int32)),
                         diff_scores(q, k_new, q_pos, q_pos)], axis=-1)
    a = diff_weights(s, lam).astype(v_new.dtype)
    return (jnp.einsum("bhqk,bkhd->bqhd", a[..., :past], v_past)
            + jnp.einsum("bhqk,bkhd->bqhd", a[..., past:], v_new))


def short_conv(bg, cg, xc, conv_w, prev):
    u = cg * xc
    u_pad = jnp.concatenate([prev.astype(u.dtype), u], axis=1)
    t = u.shape[1]
    y = conv_w[0] * u_pad[:, 0:t]
    for j in range(1, CONV_K):
        y = y + conv_w[j] * u_pad[:, j:j + t]
    return bg * y, u_pad[:, -(CONV_K - 1):]


def merge_heads(o_att, y_conv, g_sub, lam_init, w_out):
    b, t = y_conv.shape[:2]
    o = rms_norm(o_att, g_sub) * (1.0 - lam_init)
    cat = jnp.concatenate([o.reshape(b, t, ATT_WIDTH), y_conv], axis=-1)
    return cat @ w_out


def grouped_experts(x, e_idx, gate, w_gate, w_up, w_down):
    n_tok, d = x.shape
    n_assign = n_tok * TOP_K
    flat_e = e_idx.reshape(n_assign)
    flat_tok = jnp.repeat(jnp.arange(n_tok, dtype=jnp.int32), TOP_K)
    flat_w = gate.reshape(n_assign)
    order = jnp.argsort(flat_e, stable=True)
    sorted_e = flat_e[order]
    counts = jnp.bincount(flat_e, length=N_EXPERTS)
    padded = (counts + MOE_BLOCK - 1) // MOE_BLOCK * MOE_BLOCK
    pad_end = jnp.cumsum(padded)
    pad_start = pad_end - padded
    start = jnp.cumsum(counts) - counts
    dest = pad_start[sorted_e] + (jnp.arange(n_assign, dtype=jnp.int32) - start[sorted_e])
    n_blocks = -(-n_assign // MOE_BLOCK) + N_EXPERTS
    n_slots = n_blocks * MOE_BLOCK
    slot_tok = jnp.full((n_slots,), n_tok, jnp.int32).at[dest].set(flat_tok[order])
    slot_w = jnp.zeros((n_slots,), x.dtype).at[dest].set(flat_w[order])
    block_e = jnp.minimum(jnp.searchsorted(pad_end, jnp.arange(n_blocks) * MOE_BLOCK, side="right"), N_EXPERTS - 1)
    x_pad = jnp.concatenate([x, jnp.zeros((1, d), x.dtype)], axis=0)
    xb = x_pad[slot_tok].reshape(n_blocks, MOE_BLOCK, d)

    def expert_block(args):
        xblk, e = args
        return (jax.nn.silu(xblk @ w_gate[e]) * (xblk @ w_up[e])) @ w_down[e]

    yb = lax.map(expert_block, (xb, block_e)).reshape(n_slots, d) * slot_w[:, None]
    return jax.ops.segment_sum(yb, slot_tok, num_segments=n_tok + 1)[:n_tok]


def hier_moe(h, w_rg, b_rg, w_re, b_re, w_gate, w_up, w_down):
    b, t, d = h.shape
    x = h.reshape(b * t, d)
    xf = x.astype(jnp.float32)
    lg = xf @ w_rg.astype(jnp.float32) + b_rg.astype(jnp.float32)
    pg = jax.nn.softmax(lg, axis=-1)
    g_star = jnp.argmax(lg, axis=-1)
    p_group = jnp.take_along_axis(pg, g_star[:, None], axis=1)
    le = (xf @ w_re.astype(jnp.float32) + b_re.astype(jnp.float32)).reshape(b * t, N_GROUPS, EXPERTS_PER_GROUP)
    le = jnp.take_along_axis(le, g_star[:, None, None], axis=1)[:, 0]
    pe = jax.nn.softmax(le, axis=-1)
    top_p, top_i = lax.top_k(pe, TOP_K)
    w = p_group * top_p / jnp.sum(top_p, axis=-1, keepdims=True)
    e_idx = g_star[:, None].astype(jnp.int32) * EXPERTS_PER_GROUP + top_i.astype(jnp.int32)
    y = grouped_experts(x, e_idx, w.astype(x.dtype), w_gate, w_up, w_down)
    return y.reshape(b, t, d)


def setup_inputs(seed: int = 0) -> dict:
    key = jax.random.key(seed)
    ks = jax.random.split(key, 26)
    f32 = jnp.float32

    def nrm(k, shape, scale):
        return jax.random.normal(k, shape, f32) * scale

    n_pages = PAST_LEN // PAGE_SIZE
    n_used = DEC_BATCH * n_pages
    n_pool = n_used + max(1, n_used // 4)
    page_table = jax.random.permutation(ks[5], n_pool)[:n_used].reshape(DEC_BATCH, n_pages).astype(jnp.int32)
    return {
        "x_prompt": nrm(ks[0], (BATCH, SEQ, D_MODEL), 1.0),
        "x_sample": nrm(ks[1], (DEC_BATCH, DEC_SEQ, D_MODEL), 1.0),
        "cache_k": nrm(ks[2], (DEPTH, n_pool, PAGE_SIZE, N_ATT_HEADS, QK_DIM), 1.0),
        "cache_v": nrm(ks[3], (DEPTH, n_pool, PAGE_SIZE, N_ATT_HEADS, V_DIM), 1.0),
        "state_conv": nrm(ks[4], (DEPTH, DEC_BATCH, CONV_K - 1, CONV_WIDTH), 1.0),
        "page_table": page_table,
        "norm_mix": 1.0 + nrm(ks[6], (DEPTH, D_MODEL), 0.02),
        "w_in": nrm(ks[7], (DEPTH, D_MODEL, IN_COLS), D_MODEL ** -0.5),
        "g_q": 1.0 + nrm(ks[8], (DEPTH, QK_HALF), 0.02),
        "g_k": 1.0 + nrm(ks[9], (DEPTH, QK_HALF), 0.02),
        "lam_q1": nrm(ks[10], (DEPTH, QK_HALF), 0.1),
        "lam_k1": nrm(ks[11], (DEPTH, QK_HALF), 0.1),
        "lam_q2": nrm(ks[12], (DEPTH, QK_HALF), 0.1),
        "lam_k2": nrm(ks[13], (DEPTH, QK_HALF), 0.1),
        "g_sub": 1.0 + nrm(ks[14], (DEPTH, V_DIM), 0.02),
        "conv_w": nrm(ks[15], (DEPTH, CONV_K, CONV_WIDTH), CONV_K ** -0.5),
        "w_out": nrm(ks[16], (DEPTH, MIX_WIDTH, D_MODEL), MIX_WIDTH ** -0.5),
        "norm_ffn": 1.0 + nrm(ks[17], (DEPTH, D_MODEL), 0.02),
        "w_router_group": nrm(ks[18], (DEPTH, D_MODEL, N_GROUPS), D_MODEL ** -0.5),
        "b_router_group": nrm(ks[19], (DEPTH, N_GROUPS), 0.01),
        "w_router_expert": nrm(ks[20], (DEPTH, D_MODEL, N_EXPERTS), D_MODEL ** -0.5),
        "b_router_expert": nrm(ks[21], (DEPTH, N_EXPERTS), 0.01),
        "w_gate": nrm(ks[22], (DEPTH, N_EXPERTS, D_MODEL, D_EXPERT), D_MODEL ** -0.5),
        "w_up": nrm(ks[23], (DEPTH, N_EXPERTS, D_MODEL, D_EXPERT), D_MODEL ** -0.5),
        "w_down": nrm(ks[24], (DEPTH, N_EXPERTS, D_EXPERT, D_MODEL), D_EXPERT ** -0.5),
    }


def reference(x_prompt, x_sample, cache_k, cache_v, state_conv, page_table,
              norm_mix, w_in, g_q, g_k, lam_q1, lam_k1, lam_q2, lam_k2, g_sub, conv_w, w_out,
              norm_ffn, w_router_group, b_router_group, w_router_expert, b_router_expert,
              w_gate, w_up, w_down):
    xp, xs = x_prompt, x_sample
    kp, vp, cp, ksm, vsm, csm = [], [], [], [], [], []
    for l in range(DEPTH):
        lam_init = 0.8 - 0.6 * math.exp(-0.3 * l)
        lam = diff_lambda(lam_q1[l], lam_k1[l], lam_q2[l], lam_k2[l], lam_init)

        b, t = xp.shape[:2]
        q, k, v, bg, cg, xc = project(rms_norm(xp, norm_mix[l]), w_in[l], g_q[l], g_k[l])
        o = prompt_attention(q, k, v, lam)
        yc, conv_new = short_conv(bg, cg, xc, conv_w[l], jnp.zeros((b, CONV_K - 1, CONV_WIDTH), xp.dtype))
        xp = xp + merge_heads(o, yc, g_sub[l], lam_init, w_out[l])
        xp = xp + hier_moe(rms_norm(xp, norm_ffn[l]), w_router_group[l], b_router_group[l],
                           w_router_expert[l], b_router_expert[l], w_gate[l], w_up[l], w_down[l])
        kp.append(k.reshape(b, t, N_ATT_HEADS, QK_DIM))
        vp.append(v)
        cp.append(conv_new)

        nb, ts = xs.shape[:2]
        q, k, v, bg, cg, xc = project(rms_norm(xs, norm_mix[l]), w_in[l], g_q[l], g_k[l])
        o = sample_attention(q, k, v, cache_k[l, page_table], cache_v[l, page_table], lam)
        yc, conv_new = short_conv(bg, cg, xc, conv_w[l], state_conv[l])
        xs = xs + merge_heads(o, yc, g_sub[l], lam_init, w_out[l])
        xs = xs + hier_moe(rms_norm(xs, norm_ffn[l]), w_router_group[l], b_router_group[l],
                           w_router_expert[l], b_router_expert[l], w_gate[l], w_up[l], w_down[l])
        ksm.append(k.reshape(nb, ts, N_ATT_HEADS, QK_DIM))
        vsm.append(v)
        csm.append(conv_new)

    k_prompt = jnp.stack(kp, axis=0)
    v_prompt = jnp.stack(vp, axis=0)
    conv_prompt = jnp.stack(cp, axis=0)
    k_sample = jnp.stack(ksm, axis=0)
    v_sample = jnp.stack(vsm, axis=0)
    conv_sample = jnp.stack(csm, axis=0)
    return (xp, xs, k_prompt, v_prompt, conv_prompt, k_sample, v_sample, conv_sample)
```

```python
import functools
import math

import jax
import jax.numpy as jnp
from jax import lax
from jax.experimental import pallas as pl
from jax.experimental.pallas import tpu as pltpu

D_MODEL = 2048
N_HEADS = 8
V_DIM = 128
QK_HALF = 64
ATT_W = N_HEADS * V_DIM
CONV_W = D_MODEL - ATT_W
CONV_K = 3
IN_COLS = 3 * ATT_W + 3 * CONV_W
N_GROUPS = 4
EPG = 8
N_EXPERTS = N_GROUPS * EPG
D_EXPERT = D_MODEL // 4
PAGE = 128
EPS = 1e-6

LANES = 128
SUBLANES = 8
N_LT = D_MODEL // LANES
MOE_BLK = 128
NEG_BIG = -1e30

F32 = jnp.float32
BF16 = jnp.bfloat16


def _cparams(n_axes, vmem_mb):
    return pltpu.CompilerParams(
        dimension_semantics=("arbitrary",) * n_axes,
        vmem_limit_bytes=vmem_mb * 1024 * 1024)


def _const_spec(shape, single_buffer=False):
    nd = len(shape)
    idx = lambda *_: (0,) * nd
    if single_buffer:
        return pl.BlockSpec(shape, idx, pipeline_mode=pl.Buffered(1))
    return pl.BlockSpec(shape, idx)


def _dot(a, b):
    return jnp.dot(a, b, preferred_element_type=F32)


def _dot_nt(a, b):
    return lax.dot_general(a, b, (((1,), (1,)), ((), ())), preferred_element_type=F32)


def _rms_rows(x, gain):
    ms = jnp.mean(x * x, axis=-1, keepdims=True)
    return x * lax.rsqrt(ms + EPS) * gain


def _diff_lambda(lq1, lk1, lq2, lk2, lam_init):
    s1 = jnp.sum(lq1 * lk1, axis=-1, keepdims=True)
    s2 = jnp.sum(lq2 * lk2, axis=-1, keepdims=True)
    return jnp.exp(s1) - jnp.exp(s2) + lam_init


IN_CHUNK = 512


def _half_norm(p, gsum, gain):
    ms = _dot((p * p).astype(BF16), gsum)
    return p * lax.rsqrt(ms + EPS) * gain


def _in_proj_prompt_kernel(x_ref, gmix_ref, w_ref, gq_ref, gk_ref, gsum_ref, cw_ref,
                           q2_ref, kh_ref, vh_ref, kf_ref, vf_ref, yc_ref, cs_ref,
                           h_scr, u_scr, *, tm, tiles_per_seq):
    i = pl.program_id(0)

    @pl.when(i % tiles_per_seq == 0)
    def _():
        u_scr[0:SUBLANES, :] = jnp.zeros((SUBLANES, CONV_W), F32)

    h_scr[...] = _rms_rows(x_ref[...], gmix_ref[...]).astype(BF16)
    gsum = gsum_ref[...]
    lane = lax.broadcasted_iota(jnp.int32, (tm, LANES), 1)
    first_half = lane < QK_HALF
    heads_per_chunk = IN_CHUNK // V_DIM

    for c in range(ATT_W // IN_CHUNK):
        cols = slice(c * IN_CHUNK, (c + 1) * IN_CHUNK)
        p = _dot(h_scr[...], w_ref[:, c * IN_CHUNK:(c + 1) * IN_CHUNK])
        qn = _half_norm(p, gsum, gq_ref[:, cols]) * (QK_HALF ** -0.5)
        for hh in range(heads_per_chunk):
            qh = qn[:, hh * V_DIM:(hh + 1) * V_DIM]
            head = c * heads_per_chunk + hh
            q2_ref[head, 0] = jnp.where(first_half, qh, 0.0).astype(BF16)
            q2_ref[head, 1] = jnp.where(first_half, 0.0, qh).astype(BF16)
        p = _dot(h_scr[...], w_ref[:, ATT_W + c * IN_CHUNK:ATT_W + (c + 1) * IN_CHUNK])
        kn = _half_norm(p, gsum, gk_ref[:, cols])
        kf_ref[:, cols] = kn
        for hh in range(heads_per_chunk):
            kh_ref[c * heads_per_chunk + hh] = kn[:, hh * V_DIM:(hh + 1) * V_DIM].astype(BF16)
        p = _dot(h_scr[...], w_ref[:, 2 * ATT_W + c * IN_CHUNK:2 * ATT_W + (c + 1) * IN_CHUNK])
        vf_ref[:, cols] = p
        for hh in range(heads_per_chunk):
            vh_ref[c * heads_per_chunk + hh] = p[:, hh * V_DIM:(hh + 1) * V_DIM].astype(BF16)

    base = 3 * ATT_W
    for c in range(CONV_W // IN_CHUNK):
        cols = slice(c * IN_CHUNK, (c + 1) * IN_CHUNK)
        lo = c * IN_CHUNK
        bg = _dot(h_scr[...], w_ref[:, base + lo:base + lo + IN_CHUNK])
        cg = _dot(h_scr[...], w_ref[:, base + CONV_W + lo:base + CONV_W + lo + IN_CHUNK])
        xc = _dot(h_scr[...], w_ref[:, base + 2 * CONV_W + lo:base + 2 * CONV_W + lo + IN_CHUNK])
        u = cg * xc
        u_scr[SUBLANES:SUBLANES + tm, cols] = u
        u1 = u_scr[SUBLANES - 1:SUBLANES - 1 + tm, cols]
        u2 = u_scr[SUBLANES - 2:SUBLANES - 2 + tm, cols]
        y = cw_ref[0:1, cols] * u2 + cw_ref[1:2, cols] * u1 + cw_ref[2:3, cols] * u
        yc_ref[:, cols] = (bg * y).astype(BF16)

    cs_ref[...] = u_scr[tm + SUBLANES - 2:tm + SUBLANES, :]
    u_scr[0:SUBLANES, :] = u_scr[tm:tm + SUBLANES, :]


def _in_proj_sample_kernel(x_ref, gmix_ref, w_ref, gq_ref, gk_ref, gsum_ref, cw_ref, p0_ref, p1_ref,
                           q_ref, kf_ref, vf_ref, yc_ref, u_ref, h_scr):
    h_scr[...] = _rms_rows(x_ref[...], gmix_ref[...]).astype(BF16)
    gsum = gsum_ref[...]
    for c in range(ATT_W // IN_CHUNK):
        cols = slice(c * IN_CHUNK, (c + 1) * IN_CHUNK)
        p = _dot(h_scr[...], w_ref[:, c * IN_CHUNK:(c + 1) * IN_CHUNK])
        q_ref[:, cols] = (_half_norm(p, gsum, gq_ref[:, cols]) * (QK_HALF ** -0.5)).astype(BF16)
        p = _dot(h_scr[...], w_ref[:, ATT_W + c * IN_CHUNK:ATT_W + (c + 1) * IN_CHUNK])
        kf_ref[:, cols] = _half_norm(p, gsum, gk_ref[:, cols])
        vf_ref[:, cols] = _dot(h_scr[...], w_ref[:, 2 * ATT_W + c * IN_CHUNK:2 * ATT_W + (c + 1) * IN_CHUNK])
    base = 3 * ATT_W
    for c in range(CONV_W // IN_CHUNK):
        cols = slice(c * IN_CHUNK, (c + 1) * IN_CHUNK)
        lo = c * IN_CHUNK
        bg = _dot(h_scr[...], w_ref[:, base + lo:base + lo + IN_CHUNK])
        cg = _dot(h_scr[...], w_ref[:, base + CONV_W + lo:base + CONV_W + lo + IN_CHUNK])
        xc = _dot(h_scr[...], w_ref[:, base + 2 * CONV_W + lo:base + 2 * CONV_W + lo + IN_CHUNK])
        u = cg * xc
        u_ref[:, cols] = u
        y = cw_ref[0:1, cols] * p0_ref[:, cols] + cw_ref[1:2, cols] * p1_ref[:, cols] + cw_ref[2:3, cols] * u
        yc_ref[:, cols] = (bg * y).astype(BF16)


def _in_proj_prompt(x, gmix, w_bf, gq, gk, gsum, cw, *, batch, seq, tm=256):
    t = batch * seq
    tps = seq // tm
    row = lambda i: (i, 0)
    out_shape = (
        jax.ShapeDtypeStruct((batch, N_HEADS, 2, seq, V_DIM), BF16),
        jax.ShapeDtypeStruct((batch, N_HEADS, seq, V_DIM), BF16),
        jax.ShapeDtypeStruct((batch, N_HEADS, seq, V_DIM), BF16),
        jax.ShapeDtypeStruct((t, ATT_W), F32),
        jax.ShapeDtypeStruct((t, ATT_W), F32),
        jax.ShapeDtypeStruct((t, CONV_W), BF16),
        jax.ShapeDtypeStruct((batch, CONV_K - 1, CONV_W), F32),
    )
    out_specs = (
        pl.BlockSpec((None, N_HEADS, 2, tm, V_DIM), lambda i: (i // tps, 0, 0, i % tps, 0)),
        pl.BlockSpec((None, N_HEADS, tm, V_DIM), lambda i: (i // tps, 0, i % tps, 0)),
        pl.BlockSpec((None, N_HEADS, tm, V_DIM), lambda i: (i // tps, 0, i % tps, 0)),
        pl.BlockSpec((tm, ATT_W), row),
        pl.BlockSpec((tm, ATT_W), row),
        pl.BlockSpec((tm, CONV_W), row),
        pl.BlockSpec((None, CONV_K - 1, CONV_W), lambda i: (i // tps, 0, 0)),
    )
    in_specs = [
        pl.BlockSpec((tm, D_MODEL), row),
        _const_spec((1, D_MODEL)),
        _const_spec((D_MODEL, IN_COLS), single_buffer=True),
        _const_spec((1, ATT_W)),
        _const_spec((1, ATT_W)),
        _const_spec((IN_CHUNK, IN_CHUNK)),
        _const_spec((CONV_K, CONV_W)),
    ]
    return pl.pallas_call(
        functools.partial(_in_proj_prompt_kernel, tm=tm, tiles_per_seq=tps),
        out_shape=out_shape, grid=(t // tm,), in_specs=in_specs, out_specs=out_specs,
        scratch_shapes=[pltpu.VMEM((tm, D_MODEL), BF16), pltpu.VMEM((tm + SUBLANES, CONV_W), F32)],
        compiler_params=_cparams(1, 56), name="in_proj_prompt",
    )(x, gmix, w_bf, gq, gk, gsum, cw)


def _in_proj_sample(x, gmix, w_bf, gq, gk, gsum, cw, prev0, prev1):
    t = x.shape[0]
    full = lambda shape: _const_spec(shape)
    out_shape = (
        jax.ShapeDtypeStruct((t, ATT_W), BF16),
        jax.ShapeDtypeStruct((t, ATT_W), F32),
        jax.ShapeDtypeStruct((t, ATT_W), F32),
        jax.ShapeDtypeStruct((t, CONV_W), BF16),
        jax.ShapeDtypeStruct((t, CONV_W), F32),
    )
    in_specs = [
        full((t, D_MODEL)), full((1, D_MODEL)), _const_spec((D_MODEL, IN_COLS), single_buffer=True),
        full((1, ATT_W)), full((1, ATT_W)), full((IN_CHUNK, IN_CHUNK)), full((CONV_K, CONV_W)),
        full((t, CONV_W)), full((t, CONV_W)),
    ]
    return pl.pallas_call(
        _in_proj_sample_kernel, out_shape=out_shape, grid=(1,), in_specs=in_specs,
        out_specs=tuple(full(s.shape) for s in out_shape),
        scratch_shapes=[pltpu.VMEM((t, D_MODEL), BF16)],
        compiler_params=_cparams(1, 48), name="in_proj_sample",
    )(x, gmix, w_bf, gq, gk, gsum, cw, prev0, prev1)


def _prompt_attn_kernel(slopes_ref, q2_ref, k_ref, v_ref, lq1_ref, lk1_ref, lq2_ref, lk2_ref, gsub_ref,
                        o_ref, bias_scr, *, tq, lam_init):
    h = pl.program_id(1)
    i = pl.program_id(2)
    slope = slopes_ref[h]
    row = lax.broadcasted_iota(jnp.int32, (tq, tq), 0)
    col = lax.broadcasted_iota(jnp.int32, (tq, tq), 1)
    bias_scr[...] = slope * (col - row).astype(F32)
    q0 = q2_ref[0]
    q1 = q2_ref[1]

    def half_step(q, k, v, shift, m, l, acc, masked):
        s = _dot_nt(q, k) + bias_scr[...]
        if masked:
            s = jnp.where(col <= row, s, NEG_BIG)
        m_new = jnp.maximum(m, jnp.max(s, axis=-1, keepdims=True) + shift)
        alpha = jnp.exp(m - m_new)
        p = jnp.exp(s - (m_new - shift))
        l_new = alpha * l + jnp.sum(p, axis=-1, keepdims=True)
        acc_new = alpha * acc + _dot(p.astype(BF16), v)
        return m_new, l_new, acc_new

    def step(j, carry, masked):
        m0, l0, a0, m1, l1, a1 = carry
        start = pl.multiple_of(j * tq, tq)
        k = k_ref[pl.ds(start, tq), :]
        v = v_ref[pl.ds(start, tq), :]
        shift = slope * ((j - i) * tq).astype(F32)
        m0, l0, a0 = half_step(q0, k, v, shift, m0, l0, a0, masked)
        m1, l1, a1 = half_step(q1, k, v, shift, m1, l1, a1, masked)
        return m0, l0, a0, m1, l1, a1

    neg = jnp.full((tq, 1), NEG_BIG, F32)
    zero1 = jnp.zeros((tq, 1), F32)
    zacc = jnp.zeros((tq, V_DIM), F32)
    carry = lax.fori_loop(0, i, lambda j, c: step(j, c, False), (neg, zero1, zacc, neg, zero1, zacc))
    m0, l0, a0, m1, l1, a1 = step(i, carry, True)

    lam = _diff_lambda(lq1_ref[...], lk1_ref[...], lq2_ref[...], lk2_ref[...], lam_init)
    o = a0 / l0 - lam * (a1 / l1)
    o_ref[...] = (_rms_rows(o, gsub_ref[...]) * (1.0 - lam_init)).astype(BF16)


def _prompt_attention(slopes, q2, kh, vh, lq1, lk1, lq2, lk2, gsub, *, lam_init, tq=256):
    batch, _, _, seq, _ = q2.shape
    vec = lambda n: pl.BlockSpec((1, n), lambda b, h, i, s: (0, 0))
    grid_spec = pltpu.PrefetchScalarGridSpec(
        num_scalar_prefetch=1, grid=(batch, N_HEADS, seq // tq),
        in_specs=[
            pl.BlockSpec((None, None, 2, tq, V_DIM), lambda b, h, i, s: (b, h, 0, i, 0)),
            pl.BlockSpec((None, None, seq, V_DIM), lambda b, h, i, s: (b, h, 0, 0)),
            pl.BlockSpec((None, None, seq, V_DIM), lambda b, h, i, s: (b, h, 0, 0)),
            vec(QK_HALF), vec(QK_HALF), vec(QK_HALF), vec(QK_HALF), vec(V_DIM),
        ],
        out_specs=pl.BlockSpec((None, tq, V_DIM), lambda b, h, i, s: (b, i, h)),
        scratch_shapes=[pltpu.VMEM((tq, tq), F32)],
    )
    return pl.pallas_call(
        functools.partial(_prompt_attn_kernel, tq=tq, lam_init=lam_init),
        out_shape=jax.ShapeDtypeStruct((batch, seq, ATT_W), BF16),
        grid_spec=grid_spec, compiler_params=_cparams(3, 32), name="prompt_attention",
    )(slopes, q2, kh, vh, lq1, lk1, lq2, lk2, gsub)


def _sample_attn_kernel(pt_ref, *refs, n_pages, lam_init):
    del pt_ref
    k_refs = refs[:n_pages]
    v_refs = refs[n_pages:2 * n_pages]
    (q_ref, kn_ref, vn_ref, bias_ref, lq1_ref, lk1_ref, lq2_ref, lk2_ref, gsub_ref, o_ref) = refs[2 * n_pages:]
    n_rows = 2 * N_HEADS

    r_io = lax.broadcasted_iota(jnp.int32, (n_rows, ATT_W), 0)
    l_io = lax.broadcasted_iota(jnp.int32, (n_rows, ATT_W), 1)
    sel = (l_io // QK_HALF) == ((r_io % N_HEADS) * 2 + r_io // N_HEADS)
    qm = jnp.where(sel, jnp.broadcast_to(q_ref[...].astype(F32), (n_rows, ATT_W)), 0.0)
    qm_bf = qm.astype(BF16)

    s_parts = [_dot_nt(qm_bf, k_refs[p][...].astype(BF16)) for p in range(n_pages)]
    s = jnp.concatenate(s_parts, axis=-1) + bias_ref[...]
    kn = kn_ref[...].astype(BF16).astype(F32)
    s_new = jnp.sum(qm * kn, axis=-1, keepdims=True)

    m = jnp.maximum(jnp.max(s, axis=-1, keepdims=True), s_new)
    p = jnp.exp(s - m)
    p_new = jnp.exp(s_new - m)
    inv_l = 1.0 / (jnp.sum(p, axis=-1, keepdims=True) + p_new)
    lam = _diff_lambda(lq1_ref[...], lk1_ref[...], lq2_ref[...], lk2_ref[...], lam_init)
    w0 = inv_l[0:N_HEADS]
    w1 = lam * inv_l[N_HEADS:n_rows]
    a = p[0:N_HEADS] * w0 - p[N_HEADS:n_rows] * w1
    a_new = p_new[0:N_HEADS] * w0 - p_new[N_HEADS:n_rows] * w1
    a_bf = jnp.concatenate([a, jnp.zeros_like(a)], axis=0).astype(BF16)

    acc = jnp.zeros((n_rows, ATT_W), F32)
    for pg in range(n_pages):
        acc = acc + _dot(a_bf[:, pg * PAGE:(pg + 1) * PAGE], v_refs[pg][...].astype(BF16))
    vn = vn_ref[...].astype(BF16).astype(F32)
    full = acc[0:N_HEADS] + a_new.astype(BF16).astype(F32) * vn
    hrow = lax.broadcasted_iota(jnp.int32, (N_HEADS, V_DIM), 0)
    o = jnp.zeros((N_HEADS, V_DIM), F32)
    for hd in range(N_HEADS):
        o = o + jnp.where(hrow == hd, full[:, hd * V_DIM:(hd + 1) * V_DIM], 0.0)
    o_ref[...] = _rms_rows(o, gsub_ref[...]) * (1.0 - lam_init)


def _sample_attention(page_table, ck, cv, q, kn, vn, bias, lq1, lk1, lq2, lk2, gsub, *, lam_init):
    nb, n_pages = page_table.shape
    pt_flat = page_table.reshape(nb * n_pages)
    page_spec = lambda p: pl.BlockSpec((None, PAGE, ATT_W), lambda b, pt: (pt[b * n_pages + p], 0, 0))
    rowspec = pl.BlockSpec((None, 1, ATT_W), lambda b, pt: (b, 0, 0))
    vec = lambda n: pl.BlockSpec((1, n), lambda b, pt: (0, 0))
    in_specs = ([page_spec(p) for p in range(n_pages)] + [page_spec(p) for p in range(n_pages)]
                + [rowspec, rowspec, rowspec,
                   pl.BlockSpec((2 * N_HEADS, n_pages * PAGE), lambda b, pt: (0, 0)),
                   vec(QK_HALF), vec(QK_HALF), vec(QK_HALF), vec(QK_HALF), vec(V_DIM)])
    grid_spec = pltpu.PrefetchScalarGridSpec(
        num_scalar_prefetch=1, grid=(nb,), in_specs=in_specs,
        out_specs=pl.BlockSpec((None, N_HEADS, V_DIM), lambda b, pt: (b, 0, 0)))
    return pl.pallas_call(
        functools.partial(_sample_attn_kernel, n_pages=n_pages, lam_init=lam_init),
        out_shape=jax.ShapeDtypeStruct((nb, N_HEADS, V_DIM), F32),
        grid_spec=grid_spec, compiler_params=_cparams(1, 48), name="sample_attention",
    )(pt_flat, *([ck] * n_pages), *([cv] * n_pages), q, kn, vn, bias, lq1, lk1, lq2, lk2, gsub)


INFO_E1, INFO_E2, INFO_W1, INFO_W2, INFO_R1, INFO_R2 = 0, 1, 2, 3, 4, 5


def _lane_min_index(mask, lane):
    return jnp.min(jnp.where(mask, lane, float(LANES)), axis=-1, keepdims=True)


def _out_proj_kernel(o_ref, yc_ref, x_ref, w_ref, gffn_ref, wrh_ref, wrl_ref, br_ref, tri_ref, cnt_in_ref,
                     xm_ref, h2_ref, info_ref, cnt_ref, cnt_scr, *, tm):
    i = pl.program_id(0)

    @pl.when(i == 0)
    def _():
        cnt_scr[...] = cnt_in_ref[...]

    xm = x_ref[...] + _dot(o_ref[...], w_ref[0:ATT_W, :]) + _dot(yc_ref[...], w_ref[ATT_W:D_MODEL, :])
    xm_ref[...] = xm
    h2 = _rms_rows(xm, gffn_ref[...])
    for s in range(N_LT):
        h2_ref[:, s, :] = h2[:, s * LANES:(s + 1) * LANES]

    hh = h2.astype(BF16)
    hl = (h2 - hh.astype(F32)).astype(BF16)
    logits = _dot(hh, wrh_ref[...]) + _dot(hl, wrh_ref[...]) + _dot(hh, wrl_ref[...]) + br_ref[...]

    lane = lax.broadcasted_iota(jnp.int32, (tm, LANES), 1).astype(F32)
    is_grp = lane < N_GROUPS
    lg = jnp.where(is_grp, logits, NEG_BIG)
    mg = jnp.max(lg, axis=-1, keepdims=True)
    g_star = _lane_min_index(is_grp & (lg == mg), lane)
    p_group = 1.0 / jnp.sum(jnp.where(is_grp, jnp.exp(lg - mg), 0.0), axis=-1, keepdims=True)

    e_lo = N_GROUPS + g_star * EPG
    in_grp = (lane >= e_lo) & (lane < e_lo + EPG)
    le = jnp.where(in_grp, logits, NEG_BIG)
    m1 = jnp.max(le, axis=-1, keepdims=True)
    ex = jnp.where(in_grp, jnp.exp(le - m1), 0.0)
    pe = ex / jnp.sum(ex, axis=-1, keepdims=True)
    pe_m = jnp.where(in_grp, pe, -1.0)
    p1 = jnp.max(pe_m, axis=-1, keepdims=True)
    i1 = _lane_min_index(pe_m == p1, lane)
    pe_m2 = jnp.where(lane == i1, -1.0, pe_m)
    p2 = jnp.max(pe_m2, axis=-1, keepdims=True)
    i2 = _lane_min_index(pe_m2 == p2, lane)
    denom = p1 + p2
    w1 = p_group * p1 / denom
    w2 = p_group * p2 / denom
    e1 = i1 - N_GROUPS
    e2 = i2 - N_GROUPS

    oh1 = lane == e1
    oh2 = lane == e2
    oh = jnp.where(oh1 | oh2, 1.0, 0.0)
    prefix = _dot(tri_ref[...], oh.astype(BF16)) + cnt_scr[...]
    r1 = jnp.sum(jnp.where(oh1, prefix, 0.0), axis=-1, keepdims=True)
    r2 = jnp.sum(jnp.where(oh2, prefix, 0.0), axis=-1, keepdims=True)
    cnt_scr[...] = cnt_scr[...] + jnp.sum(oh, axis=0, keepdims=True)
    cnt_ref[...] = cnt_scr[...]

    info = jnp.zeros((tm, LANES), F32)
    for idx, val in ((INFO_E1, e1), (INFO_E2, e2), (INFO_W1, w1), (INFO_W2, w2),
                     (INFO_R1, r1), (INFO_R2, r2)):
        info = jnp.where(lane == idx, val, info)
    info_ref[...] = info


def _out_proj(o, yc, x, w_bf, gffn, wrh, wrl, br, tri, cnt_in, *, tm):
    t = x.shape[0]
    row = lambda i: (i, 0)
    in_specs = [
        pl.BlockSpec((tm, ATT_W), row), pl.BlockSpec((tm, CONV_W), row), pl.BlockSpec((tm, D_MODEL), row),
        _const_spec((D_MODEL, D_MODEL)), _const_spec((1, D_MODEL)),
        _const_spec((D_MODEL, LANES)), _const_spec((D_MODEL, LANES)), _const_spec((1, LANES)),
        _const_spec((tm, tm)), _const_spec((1, LANES)),
    ]
    out_specs = (
        pl.BlockSpec((tm, D_MODEL), row),
        pl.BlockSpec((tm, N_LT, LANES), lambda i: (i, 0, 0)),
        pl.BlockSpec((tm, LANES), row),
        _const_spec((1, LANES)),
    )
    out_shape = (
        jax.ShapeDtypeStruct((t, D_MODEL), F32), jax.ShapeDtypeStruct((t, N_LT, LANES), F32),
        jax.ShapeDtypeStruct((t, LANES), F32), jax.ShapeDtypeStruct((1, LANES), F32),
    )
    return pl.pallas_call(
        functools.partial(_out_proj_kernel, tm=tm),
        out_shape=out_shape, grid=(t // tm,), in_specs=in_specs, out_specs=out_specs,
        scratch_shapes=[pltpu.VMEM((1, LANES), F32)],
        compiler_params=_cparams(1, 48), name="out_proj",
    )(o, yc, x, w_bf, gffn, wrh, wrl, br, tri, cnt_in)


def _dispatch_kernel(d1_ref, d2_ref, h2_hbm, xb_in, xb_hbm, sem, *, tm):
    del xb_in
    base = pl.program_id(0) * tm

    def issue(r, c):
        t = base + r
        pltpu.make_async_copy(h2_hbm.at[t], xb_hbm.at[d1_ref[t]], sem).start()
        pltpu.make_async_copy(h2_hbm.at[t], xb_hbm.at[d2_ref[t]], sem).start()
        return c

    lax.fori_loop(0, tm, issue, 0)

    def drain(r, c):
        pltpu.make_async_copy(h2_hbm.at[0], xb_hbm.at[0], sem).wait()
        pltpu.make_async_copy(h2_hbm.at[0], xb_hbm.at[0], sem).wait()
        return c

    lax.fori_loop(0, tm, drain, 0)


def _dispatch(dest1, dest2, h2_buf, xb_init, *, tm):
    n_tok = h2_buf.shape[0]
    grid_spec = pltpu.PrefetchScalarGridSpec(
        num_scalar_prefetch=2, grid=(n_tok // tm,),
        in_specs=[pl.BlockSpec(memory_space=pl.ANY), pl.BlockSpec(memory_space=pl.ANY)],
        out_specs=pl.BlockSpec(memory_space=pl.ANY),
        scratch_shapes=[pltpu.SemaphoreType.DMA(())])
    return pl.pallas_call(
        functools.partial(_dispatch_kernel, tm=tm),
        out_shape=jax.ShapeDtypeStruct(xb_init.shape, F32), grid_spec=grid_spec,
        input_output_aliases={3: 0},
        compiler_params=_cparams(1, 16), name="moe_dispatch",
    )(dest1, dest2, h2_buf, xb_init)


def _experts_kernel(be_ref, na_ref, xb_ref, wg_ref, wu_ref, wd_ref, yb_ref, x_scr):
    del be_ref

    @pl.when(pl.program_id(0) < na_ref[0])
    def _():
        for s in range(N_LT):
            x_scr[:, s * LANES:(s + 1) * LANES] = xb_ref[:, s, :].astype(BF16)
        x = x_scr[...]
        g = _dot(x, wg_ref[...])
        u = _dot(x, wu_ref[...])
        a = (g * (1.0 / (1.0 + jnp.exp(-g))) * u).astype(BF16)
        y = _dot(a, wd_ref[...])
        for s in range(N_LT):
            yb_ref[:, s, :] = y[:, s * LANES:(s + 1) * LANES]

    @pl.when(pl.program_id(0) >= na_ref[0])
    def _():
        yb_ref[...] = jnp.zeros(yb_ref.shape, F32)


def _experts(block_e, n_active, xb, wg, wu, wd):
    n_slots = xb.shape[0]
    n_blocks = n_slots // MOE_BLK
    blk = lambda b, be, na: (jnp.minimum(b, na[0] - 1), 0, 0)
    wsel = lambda b, be, na: (be[b], 0, 0)
    grid_spec = pltpu.PrefetchScalarGridSpec(
        num_scalar_prefetch=2, grid=(n_blocks,),
        in_specs=[
            pl.BlockSpec((MOE_BLK, N_LT, LANES), blk),
            pl.BlockSpec((None, D_MODEL, D_EXPERT), wsel),
            pl.BlockSpec((None, D_MODEL, D_EXPERT), wsel),
            pl.BlockSpec((None, D_EXPERT, D_MODEL), wsel),
        ],
        out_specs=pl.BlockSpec((MOE_BLK, N_LT, LANES), lambda b, be, na: (b, 0, 0)),
        scratch_shapes=[pltpu.VMEM((MOE_BLK, D_MODEL), BF16)])
    return pl.pallas_call(
        _experts_kernel, out_shape=jax.ShapeDtypeStruct(xb.shape, F32), grid_spec=grid_spec,
        compiler_params=_cparams(1, 40), name="moe_experts",
    )(block_e, n_active, xb, wg, wu, wd)


def _combine_kernel(d1_ref, d2_ref, xm_ref, info_ref, yb_hbm, out_ref, a_scr, b_scr, sem, *, tm):
    base = pl.program_id(0) * tm

    def issue(r, c):
        t = base + r
        pltpu.make_async_copy(yb_hbm.at[d1_ref[t]], a_scr.at[r], sem).start()
        pltpu.make_async_copy(yb_hbm.at[d2_ref[t]], b_scr.at[r], sem).start()
        return c

    lax.fori_loop(0, tm, issue, 0)

    def drain(r, c):
        pltpu.make_async_copy(yb_hbm.at[0], a_scr.at[0], sem).wait()
        pltpu.make_async_copy(yb_hbm.at[0], b_scr.at[0], sem).wait()
        return c

    lax.fori_loop(0, tm, drain, 0)

    info = info_ref[...]
    w1 = info[:, INFO_W1:INFO_W1 + 1]
    w2 = info[:, INFO_W2:INFO_W2 + 1]
    for s in range(N_LT):
        cols = slice(s * LANES, (s + 1) * LANES)
        out_ref[:, cols] = xm_ref[:, cols] + (a_scr[:, s, :] * w1 + b_scr[:, s, :] * w2)


def _combine(dest1, dest2, xm_buf, info_buf, yb, *, tm):
    n_rows = xm_buf.shape[0]
    grid_spec = pltpu.PrefetchScalarGridSpec(
        num_scalar_prefetch=2, grid=(n_rows // tm,),
        in_specs=[
            pl.BlockSpec((tm, D_MODEL), lambda i, a, b: (i, 0)),
            pl.BlockSpec((tm, LANES), lambda i, a, b: (i, 0)),
            pl.BlockSpec(memory_space=pl.ANY),
        ],
        out_specs=pl.BlockSpec((tm, D_MODEL), lambda i, a, b: (i, 0)),
        scratch_shapes=[pltpu.VMEM((tm, N_LT, LANES), F32), pltpu.VMEM((tm, N_LT, LANES), F32),
                        pltpu.SemaphoreType.DMA(())])
    return pl.pallas_call(
        functools.partial(_combine_kernel, tm=tm),
        out_shape=jax.ShapeDtypeStruct((n_rows, D_MODEL), F32), grid_spec=grid_spec,
        compiler_params=_cparams(1, 32), name="moe_combine",
    )(dest1, dest2, xm_buf, info_buf, yb)


def _layer(l, xp, xs, cache_k, cache_v, state_conv, page_table, norm_mix, w_in, g_q, g_k,
           lam_q1, lam_k1, lam_q2, lam_k2, g_sub, conv_w, w_out, norm_ffn, w_rg, b_rg, w_re, b_re,
           w_gate, w_up, w_down):
    batch, seq, _ = xp.shape
    nb = xs.shape[0]
    n_pages = page_table.shape[1]
    past = n_pages * PAGE
    t_p = batch * seq
    n_tok = t_p + nb
    lam_init = 0.8 - 0.6 * math.exp(-0.3 * l)

    w_in_bf = w_in[l].astype(BF16)
    w_out_bf = w_out[l].astype(BF16)
    gmix = norm_mix[l].reshape(1, D_MODEL)
    gffn = norm_ffn[l].reshape(1, D_MODEL)
    gq = jnp.tile(g_q[l], ATT_W // QK_HALF).reshape(1, ATT_W)
    gk = jnp.tile(g_k[l], ATT_W // QK_HALF).reshape(1, ATT_W)
    gsub = g_sub[l].reshape(1, V_DIM)
    grp = jnp.arange(IN_CHUNK) // QK_HALF
    gsum = jnp.where(grp[:, None] == grp[None, :], 1.0 / QK_HALF, 0.0).astype(BF16)
    lq1, lk1, lq2, lk2 = (v[l].reshape(1, QK_HALF) for v in (lam_q1, lam_k1, lam_q2, lam_k2))
    slopes = 2.0 ** (-8.0 * jnp.arange(1, N_HEADS + 1, dtype=F32) / N_HEADS)
    w_r = jnp.zeros((D_MODEL, LANES), F32)
    w_r = w_r.at[:, 0:N_GROUPS].set(w_rg[l]).at[:, N_GROUPS:N_GROUPS + N_EXPERTS].set(w_re[l])
    wrh = w_r.astype(BF16)
    wrl = (w_r - wrh.astype(F32)).astype(BF16)
    br = jnp.zeros((1, LANES), F32)
    br = br.at[0, 0:N_GROUPS].set(b_rg[l]).at[0, N_GROUPS:N_GROUPS + N_EXPERTS].set(b_re[l])
    wg_bf, wu_bf, wd_bf = w_gate[l].astype(BF16), w_up[l].astype(BF16), w_down[l].astype(BF16)

    x2p = xp.reshape(t_p, D_MODEL)
    x2s = xs.reshape(nb, D_MODEL)
    q2, kh, vh, kf_p, vf_p, yc_p, cs_p = _in_proj_prompt(
        x2p, gmix, w_in_bf, gq, gk, gsum, conv_w[l], batch=batch, seq=seq)
    o_p = _prompt_attention(slopes, q2, kh, vh, lq1, lk1, lq2, lk2, gsub, lam_init=lam_init)

    prev0 = state_conv[l, :, 0, :]
    prev1 = state_conv[l, :, 1, :]
    q_s, kf_s, vf_s, yc_s, u_s = _in_proj_sample(x2s, gmix, w_in_bf, gq, gk, gsum, conv_w[l], prev0, prev1)
    n_pool = cache_k.shape[1]
    ck = cache_k[l].reshape(n_pool, PAGE, ATT_W)
    cv = cache_v[l].reshape(n_pool, PAGE, ATT_W)
    dist = (past - jnp.arange(past, dtype=jnp.int32)).astype(F32)
    bias = -(jnp.tile(slopes, 2)[:, None] * dist[None, :])
    o_s = _sample_attention(page_table, ck, cv, q_s.reshape(nb, 1, ATT_W), kf_s.reshape(nb, 1, ATT_W),
                            vf_s.reshape(nb, 1, ATT_W), bias, lq1, lk1, lq2, lk2, gsub, lam_init=lam_init)
    o_s = o_s.reshape(nb, ATT_W).astype(BF16)

    cnt0 = jnp.zeros((1, LANES), F32)
    tm_p, tm_s = 256, nb
    tri = lambda n: (jnp.arange(n)[:, None] > jnp.arange(n)[None, :]).astype(BF16)
    xm_p, h2_p, info_p, cnt1 = _out_proj(
        o_p.reshape(t_p, ATT_W), yc_p, x2p, w_out_bf, gffn, wrh, wrl, br, tri(tm_p), cnt0, tm=tm_p)
    xm_s, h2_s, info_s, cnt2 = _out_proj(
        o_s, yc_s, x2s, w_out_bf, gffn, wrh, wrl, br, tri(tm_s), cnt1, tm=tm_s)

    counts = cnt2[0, 0:N_EXPERTS].astype(jnp.int32)
    padded = (counts + MOE_BLK - 1) // MOE_BLK * MOE_BLK
    pad_end = jnp.cumsum(padded)
    pad_start = pad_end - padded
    n_blocks = -(-2 * n_tok // MOE_BLK) + N_EXPERTS
    block_e = jnp.minimum(jnp.searchsorted(pad_end, jnp.arange(n_blocks, dtype=jnp.int32) * MOE_BLK, side="right"),
                          N_EXPERTS - 1).astype(jnp.int32)
    n_active = (pad_end[-1:] // MOE_BLK).astype(jnp.int32)

    def slots(info):
        e1 = info[:, INFO_E1].astype(jnp.int32)
        e2 = info[:, INFO_E2].astype(jnp.int32)
        return (pad_start[e1] + info[:, INFO_R1].astype(jnp.int32),
                pad_start[e2] + info[:, INFO_R2].astype(jnp.int32))

    d1_p, d2_p = slots(info_p)
    d1_s, d2_s = slots(info_s)

    xb = jnp.zeros((n_blocks * MOE_BLK, N_LT, LANES), F32)
    xb = _dispatch(d1_p, d2_p, h2_p, xb, tm=1024)
    xb = _dispatch(d1_s, d2_s, h2_s, xb, tm=nb)
    yb = _experts(block_e, n_active, xb, wg_bf, wu_bf, wd_bf)
    y_p = _combine(d1_p, d2_p, xm_p, info_p, yb, tm=128)
    y_s = _combine(d1_s, d2_s, xm_s, info_s, yb, tm=128)

    conv_s = jnp.stack([prev1, u_s], axis=1)
    return (y_p.reshape(batch, seq, D_MODEL), y_s.reshape(nb, 1, D_MODEL),
            kf_p.reshape(batch, seq, N_HEADS, V_DIM), vf_p.reshape(batch, seq, N_HEADS, V_DIM), cs_p,
            kf_s.reshape(nb, 1, N_HEADS, V_DIM), vf_s.reshape(nb, 1, N_HEADS, V_DIM), conv_s)


def kernel(x_prompt, x_sample, cache_k, cache_v, state_conv, page_table, norm_mix, w_in, g_q, g_k,
           lam_q1, lam_k1, lam_q2, lam_k2, g_sub, conv_w, w_out, norm_ffn, w_router_group, b_router_group,
           w_router_expert, b_router_expert, w_gate, w_up, w_down):
    depth = w_in.shape[0]
    xp, xs = x_prompt, x_sample
    outs = [[] for _ in range(6)]
    for l in range(depth):
        xp, xs, *rest = _layer(
            l, xp, xs, cache_k, cache_v, state_conv, page_table, norm_mix, w_in, g_q, g_k,
            lam_q1, lam_k1, lam_q2, lam_k2, g_sub, conv_w, w_out, norm_ffn, w_router_group, b_router_group,
            w_router_expert, b_router_expert, w_gate, w_up, w_down)
        for acc, val in zip(outs, rest):
            acc.append(val)
    return (xp, xs) + tuple(jnp.stack(v, axis=0) for v in outs)
```

```python
import functools
import math

import jax
import jax.numpy as jnp
from jax import lax
from jax.experimental import pallas as pl
from jax.experimental.pallas import tpu as pltpu

D_MODEL = 2048
N_HEADS = 8
V_DIM = 128
QK_HALF = 64
ATT_W = N_HEADS * V_DIM
CONV_W = D_MODEL - ATT_W
CONV_K = 3
IN_COLS = 3 * ATT_W + 3 * CONV_W
N_GROUPS = 4
EPG = 8
N_EXPERTS = N_GROUPS * EPG
D_EXPERT = D_MODEL // 4
PAGE = 128
EPS = 1e-6

LANES = 128
SUBLANES = 8
N_LT = D_MODEL // LANES
MOE_BLK = 128
NEG_BIG = -1e30
POS_SPLIT = 16
PAGE_ROWS = PAGE * N_HEADS

F32 = jnp.float32
BF16 = jnp.bfloat16


def _cparams(n_axes, vmem_mb):
    return pltpu.CompilerParams(
        dimension_semantics=("arbitrary",) * n_axes,
        vmem_limit_bytes=vmem_mb * 1024 * 1024)


def _const_spec(shape, single_buffer=False):
    nd = len(shape)
    idx = lambda *_: (0,) * nd
    if single_buffer:
        return pl.BlockSpec(shape, idx, pipeline_mode=pl.Buffered(1))
    return pl.BlockSpec(shape, idx)


def _dot(a, b):
    return jnp.dot(a, b, preferred_element_type=F32)


def _dot_nt(a, b):
    return lax.dot_general(a, b, (((1,), (1,)), ((), ())), preferred_element_type=F32)


def _rms_rows(x, gain):
    ms = jnp.mean(x * x, axis=-1, keepdims=True)
    return x * lax.rsqrt(ms + EPS) * gain


def _diff_lambda(lq1, lk1, lq2, lk2, lam_init):
    s1 = jnp.sum(lq1 * lk1, axis=-1, keepdims=True)
    s2 = jnp.sum(lq2 * lk2, axis=-1, keepdims=True)
    return jnp.exp(s1) - jnp.exp(s2) + lam_init


IN_CHUNK = 512


def _half_norm(p, gsum, gain):
    ms = _dot((p * p).astype(BF16), gsum)
    return p * lax.rsqrt(ms + EPS) * gain


def _in_proj_prompt_kernel(x_ref, gmix_ref, w_ref, gq_ref, gk_ref, gsum_ref, cw_ref, qal_ref,
                           q2_ref, k2_ref, vh_ref, kf_ref, vf_ref, yc_ref, cs_ref,
                           h_scr, u_scr, *, tm, tiles_per_seq):
    i = pl.program_id(0)

    @pl.when(i % tiles_per_seq == 0)
    def _():
        u_scr[0:SUBLANES, :] = jnp.zeros((SUBLANES, CONV_W), F32)

    h_scr[...] = _rms_rows(x_ref[...], gmix_ref[...]).astype(BF16)
    gsum = gsum_ref[...]
    lane = lax.broadcasted_iota(jnp.int32, (tm, LANES), 1)
    first_half = lane < QK_HALF
    heads_per_chunk = IN_CHUNK // V_DIM

    pos = lax.broadcasted_iota(jnp.int32, (tm, LANES), 0) + (i % tiles_per_seq) * tm
    pos_hi = (pos // POS_SPLIT * POS_SPLIT).astype(F32)
    pos_lo = (pos % POS_SPLIT).astype(F32)
    kal_a = jnp.where(lane == QK_HALF, pos_hi, jnp.where(lane == QK_HALF + 1, pos_lo, 0.0))
    kal_b = jnp.where(lane == 0, pos_hi, jnp.where(lane == 1, pos_lo, 0.0))

    for c in range(ATT_W // IN_CHUNK):
        cols = slice(c * IN_CHUNK, (c + 1) * IN_CHUNK)
        p = _dot(h_scr[...], w_ref[:, c * IN_CHUNK:(c + 1) * IN_CHUNK])
        qn = _half_norm(p, gsum, gq_ref[:, cols]) * (QK_HALF ** -0.5)
        for hh in range(heads_per_chunk):
            qh = qn[:, hh * V_DIM:(hh + 1) * V_DIM]
            head = c * heads_per_chunk + hh
            q2_ref[head, 0] = jnp.where(first_half, qh, qal_ref[head, 0:1, :]).astype(BF16)
            q2_ref[head, 1] = jnp.where(first_half, qal_ref[head, 1:2, :], qh).astype(BF16)
        p = _dot(h_scr[...], w_ref[:, ATT_W + c * IN_CHUNK:ATT_W + (c + 1) * IN_CHUNK])
        kn = _half_norm(p, gsum, gk_ref[:, cols])
        kf_ref[:, cols] = kn
        for hh in range(heads_per_chunk):
            kh = kn[:, hh * V_DIM:(hh + 1) * V_DIM]
            head = c * heads_per_chunk + hh
            k2_ref[head, 0] = jnp.where(first_half, kh, kal_a).astype(BF16)
            k2_ref[head, 1] = jnp.where(first_half, kal_b, kh).astype(BF16)
        p = _dot(h_scr[...], w_ref[:, 2 * ATT_W + c * IN_CHUNK:2 * ATT_W + (c + 1) * IN_CHUNK])
        vf_ref[:, cols] = p
        for hh in range(heads_per_chunk):
            vh_ref[c * heads_per_chunk + hh] = p[:, hh * V_DIM:(hh + 1) * V_DIM].astype(BF16)

    base = 3 * ATT_W
    for c in range(CONV_W // IN_CHUNK):
        cols = slice(c * IN_CHUNK, (c + 1) * IN_CHUNK)
        lo = c * IN_CHUNK
        bg = _dot(h_scr[...], w_ref[:, base + lo:base + lo + IN_CHUNK])
        cg = _dot(h_scr[...], w_ref[:, base + CONV_W + lo:base + CONV_W + lo + IN_CHUNK])
        xc = _dot(h_scr[...], w_ref[:, base + 2 * CONV_W + lo:base + 2 * CONV_W + lo + IN_CHUNK])
        u = cg * xc
        u_scr[SUBLANES:SUBLANES + tm, cols] = u
        u1 = u_scr[SUBLANES - 1:SUBLANES - 1 + tm, cols]
        u2 = u_scr[SUBLANES - 2:SUBLANES - 2 + tm, cols]
        y = cw_ref[0:1, cols] * u2 + cw_ref[1:2, cols] * u1 + cw_ref[2:3, cols] * u
        yc_ref[:, cols] = (bg * y).astype(BF16)

    cs_ref[...] = u_scr[tm + SUBLANES - 2:tm + SUBLANES, :]
    u_scr[0:SUBLANES, :] = u_scr[tm:tm + SUBLANES, :]


def _in_proj_sample_kernel(x_ref, gmix_ref, w_ref, gq_ref, gk_ref, gsum_ref, cw_ref, p0_ref, p1_ref,
                           q_ref, kf_ref, vf_ref, yc_ref, u_ref, h_scr):
    h_scr[...] = _rms_rows(x_ref[...], gmix_ref[...]).astype(BF16)
    gsum = gsum_ref[...]
    heads_per_chunk = IN_CHUNK // V_DIM
    for c in range(ATT_W // IN_CHUNK):
        cols = slice(c * IN_CHUNK, (c + 1) * IN_CHUNK)
        p = _dot(h_scr[...], w_ref[:, c * IN_CHUNK:(c + 1) * IN_CHUNK])
        qn = _half_norm(p, gsum, gq_ref[:, cols]) * (QK_HALF ** -0.5)
        p = _dot(h_scr[...], w_ref[:, ATT_W + c * IN_CHUNK:ATT_W + (c + 1) * IN_CHUNK])
        kn = _half_norm(p, gsum, gk_ref[:, cols])
        vn = _dot(h_scr[...], w_ref[:, 2 * ATT_W + c * IN_CHUNK:2 * ATT_W + (c + 1) * IN_CHUNK])
        for hh in range(heads_per_chunk):
            head = c * heads_per_chunk + hh
            hcols = slice(hh * V_DIM, (hh + 1) * V_DIM)
            q_ref[:, head, :] = qn[:, hcols]
            kf_ref[:, head, :] = kn[:, hcols]
            vf_ref[:, head, :] = vn[:, hcols]
    base = 3 * ATT_W
    for c in range(CONV_W // IN_CHUNK):
        cols = slice(c * IN_CHUNK, (c + 1) * IN_CHUNK)
        lo = c * IN_CHUNK
        bg = _dot(h_scr[...], w_ref[:, base + lo:base + lo + IN_CHUNK])
        cg = _dot(h_scr[...], w_ref[:, base + CONV_W + lo:base + CONV_W + lo + IN_CHUNK])
        xc = _dot(h_scr[...], w_ref[:, base + 2 * CONV_W + lo:base + 2 * CONV_W + lo + IN_CHUNK])
        u = cg * xc
        u_ref[:, cols] = u
        y = cw_ref[0:1, cols] * p0_ref[:, cols] + cw_ref[1:2, cols] * p1_ref[:, cols] + cw_ref[2:3, cols] * u
        yc_ref[:, cols] = (bg * y).astype(BF16)


def _in_proj_prompt(x, gmix, w_bf, gq, gk, gsum, cw, qal, *, batch, seq, tm=256):
    t = batch * seq
    tps = seq // tm
    row = lambda i: (i, 0)
    out_shape = (
        jax.ShapeDtypeStruct((batch, N_HEADS, 2, seq, V_DIM), BF16),
        jax.ShapeDtypeStruct((batch, N_HEADS, 2, seq, V_DIM), BF16),
        jax.ShapeDtypeStruct((batch, N_HEADS, seq, V_DIM), BF16),
        jax.ShapeDtypeStruct((t, ATT_W), F32),
        jax.ShapeDtypeStruct((t, ATT_W), F32),
        jax.ShapeDtypeStruct((t, CONV_W), BF16),
        jax.ShapeDtypeStruct((batch, CONV_K - 1, CONV_W), F32),
    )
    out_specs = (
        pl.BlockSpec((None, N_HEADS, 2, tm, V_DIM), lambda i: (i // tps, 0, 0, i % tps, 0)),
        pl.BlockSpec((None, N_HEADS, 2, tm, V_DIM), lambda i: (i // tps, 0, 0, i % tps, 0)),
        pl.BlockSpec((None, N_HEADS, tm, V_DIM), lambda i: (i // tps, 0, i % tps, 0)),
        pl.BlockSpec((tm, ATT_W), row),
        pl.BlockSpec((tm, ATT_W), row),
        pl.BlockSpec((tm, CONV_W), row),
        pl.BlockSpec((None, CONV_K - 1, CONV_W), lambda i: (i // tps, 0, 0)),
    )
    in_specs = [
        pl.BlockSpec((tm, D_MODEL), row),
        _const_spec((1, D_MODEL)),
        _const_spec((D_MODEL, IN_COLS), single_buffer=True),
        _const_spec((1, ATT_W)),
        _const_spec((1, ATT_W)),
        _const_spec((IN_CHUNK, IN_CHUNK)),
        _const_spec((CONV_K, CONV_W)),
        _const_spec((N_HEADS, 2, LANES)),
    ]
    return pl.pallas_call(
        functools.partial(_in_proj_prompt_kernel, tm=tm, tiles_per_seq=tps),
        out_shape=out_shape, grid=(t // tm,), in_specs=in_specs, out_specs=out_specs,
        scratch_shapes=[pltpu.VMEM((tm, D_MODEL), BF16), pltpu.VMEM((tm + SUBLANES, CONV_W), F32)],
        compiler_params=_cparams(1, 56), name="in_proj_prompt",
    )(x, gmix, w_bf, gq, gk, gsum, cw, qal)


def _in_proj_sample(x, gmix, w_bf, gq, gk, gsum, cw, prev0, prev1):
    t = x.shape[0]
    full = lambda shape: _const_spec(shape)
    out_shape = (
        jax.ShapeDtypeStruct((t, N_HEADS, V_DIM), F32),
        jax.ShapeDtypeStruct((t, N_HEADS, V_DIM), F32),
        jax.ShapeDtypeStruct((t, N_HEADS, V_DIM), F32),
        jax.ShapeDtypeStruct((t, CONV_W), BF16),
        jax.ShapeDtypeStruct((t, CONV_W), F32),
    )
    in_specs = [
        full((t, D_MODEL)), full((1, D_MODEL)), _const_spec((D_MODEL, IN_COLS), single_buffer=True),
        full((1, ATT_W)), full((1, ATT_W)), full((IN_CHUNK, IN_CHUNK)), full((CONV_K, CONV_W)),
        full((t, CONV_W)), full((t, CONV_W)),
    ]
    return pl.pallas_call(
        _in_proj_sample_kernel, out_shape=out_shape, grid=(1,), in_specs=in_specs,
        out_specs=tuple(full(s.shape) for s in out_shape),
        scratch_shapes=[pltpu.VMEM((t, D_MODEL), BF16)],
        compiler_params=_cparams(1, 48), name="in_proj_sample",
    )(x, gmix, w_bf, gq, gk, gsum, cw, prev0, prev1)


def _prompt_attn_kernel(q2_ref, k2_ref, v_ref, lq1_ref, lk1_ref, lq2_ref, lk2_ref, gsub_ref,
                        o_ref, s_scr, *, tq, lam_init):
    i = pl.program_id(2)
    n_lt = tq // LANES
    row = lax.broadcasted_iota(jnp.int32, (tq, tq), 0)
    col = lax.broadcasted_iota(jnp.int32, (tq, tq), 1)
    causal = col <= row

    def lane_tile_max(m_acc, s):
        for t in range(n_lt):
            m_acc = jnp.maximum(m_acc, s[:, t * LANES:(t + 1) * LANES])
        return m_acc

    def lane_tile_sum(l_acc, p):
        for t in range(n_lt):
            l_acc = l_acc + p[:, t * LANES:(t + 1) * LANES]
        return l_acc

    def scores(j, c):
        start = pl.multiple_of(j * tq, tq)
        return _dot_nt(q2_ref[c], k2_ref[c, pl.ds(start, tq), :])

    def pass1(j, carry):
        out = []
        for c in range(2):
            s = scores(j, c)
            s_scr[c, j] = s
            out.append(lane_tile_max(carry[c], s))
        return tuple(out)

    neg = jnp.full((tq, LANES), NEG_BIG, F32)
    m_acc = lax.fori_loop(0, i, pass1, (neg, neg))
    m_rows = []
    for c in range(2):
        s = jnp.where(causal, scores(i, c), NEG_BIG)
        s_scr[c, i] = s
        m_rows.append(jnp.max(lane_tile_max(m_acc[c], s), axis=-1, keepdims=True))

    def pass2(j, carry):
        start = pl.multiple_of(j * tq, tq)
        v = v_ref[pl.ds(start, tq), :]
        out = []
        for c in range(2):
            l_acc, acc = carry[c]
            p = jnp.exp(s_scr[c, j] - m_rows[c])
            out.append((lane_tile_sum(l_acc, p), acc + _dot(p.astype(BF16), v)))
        return tuple(out)

    z = jnp.zeros((tq, LANES), F32)
    (l0, a0), (l1, a1) = lax.fori_loop(0, i + 1, pass2, ((z, z), (z, z)))
    lam = _diff_lambda(lq1_ref[...], lk1_ref[...], lq2_ref[...], lk2_ref[...], lam_init)
    o = a0 / jnp.sum(l0, axis=-1, keepdims=True) - lam * (a1 / jnp.sum(l1, axis=-1, keepdims=True))
    o_ref[...] = (_rms_rows(o, gsub_ref[...]) * (1.0 - lam_init)).astype(BF16)


def _prompt_attention(q2, k2, vh, lq1, lk1, lq2, lk2, gsub, *, lam_init, tq=512):
    batch, _, _, seq, _ = q2.shape
    nq = seq // tq
    vec = lambda n: pl.BlockSpec((1, n), lambda b, h, i: (0, 0))
    return pl.pallas_call(
        functools.partial(_prompt_attn_kernel, tq=tq, lam_init=lam_init),
        out_shape=jax.ShapeDtypeStruct((batch, seq, ATT_W), BF16),
        grid=(batch, N_HEADS, nq),
        in_specs=[
            pl.BlockSpec((None, None, 2, tq, V_DIM), lambda b, h, i: (b, h, 0, i, 0)),
            pl.BlockSpec((None, None, 2, seq, V_DIM), lambda b, h, i: (b, h, 0, 0, 0)),
            pl.BlockSpec((None, None, seq, V_DIM), lambda b, h, i: (b, h, 0, 0)),
            vec(QK_HALF), vec(QK_HALF), vec(QK_HALF), vec(QK_HALF), vec(V_DIM),
        ],
        out_specs=pl.BlockSpec((None, tq, V_DIM), lambda b, h, i: (b, i, h)),
        scratch_shapes=[pltpu.VMEM((2, nq, tq, tq), F32)],
        compiler_params=_cparams(3, 40), name="prompt_attention",
    )(q2, k2, vh, lq1, lk1, lq2, lk2, gsub)


def _sample_attn_kernel(pt_ref, *refs, n_pages, lam_init):
    del pt_ref
    k_refs = refs[:n_pages]
    v_refs = refs[n_pages:2 * n_pages]
    (q_ref, kn_ref, vn_ref, bm_ref, lq1_ref, lk1_ref, lq2_ref, lk2_ref, gsub_ref, o_ref) = refs[2 * n_pages:]
    n_rows = 2 * N_HEADS

    q8 = q_ref[...]
    lane = lax.broadcasted_iota(jnp.int32, (N_HEADS, V_DIM), 1)
    qm = jnp.concatenate([jnp.where(lane < QK_HALF, q8, 0.0), jnp.where(lane < QK_HALF, 0.0, q8)], axis=0)
    qm_bf = qm.astype(BF16)

    s_parts = [_dot_nt(qm_bf, k_refs[p][...].reshape(PAGE_ROWS, V_DIM).astype(BF16)) for p in range(n_pages)]
    s = jnp.concatenate(s_parts, axis=-1) + bm_ref[...]
    kn = kn_ref[...].astype(BF16).astype(F32)
    s_new = jnp.sum(qm_bf.astype(F32) * jnp.concatenate([kn, kn], axis=0), axis=-1, keepdims=True)

    m = jnp.maximum(jnp.max(s, axis=-1, keepdims=True), s_new)
    p = jnp.exp(s - m)
    p_new = jnp.exp(s_new - m)
    inv_l = 1.0 / (jnp.sum(p, axis=-1, keepdims=True) + p_new)
    lam = _diff_lambda(lq1_ref[...], lk1_ref[...], lq2_ref[...], lk2_ref[...], lam_init)
    w0 = inv_l[0:N_HEADS]
    w1 = lam * inv_l[N_HEADS:n_rows]
    a = p[0:N_HEADS] * w0 - p[N_HEADS:n_rows] * w1
    a_new = p_new[0:N_HEADS] * w0 - p_new[N_HEADS:n_rows] * w1
    a_bf = jnp.concatenate([a, jnp.zeros_like(a)], axis=0).astype(BF16)

    acc = jnp.zeros((n_rows, V_DIM), F32)
    for pg in range(n_pages):
        acc = acc + _dot(a_bf[:, pg * PAGE_ROWS:(pg + 1) * PAGE_ROWS],
                         v_refs[pg][...].reshape(PAGE_ROWS, V_DIM).astype(BF16))
    vn = vn_ref[...].astype(BF16).astype(F32)
    o = acc[0:N_HEADS] + a_new.astype(BF16).astype(F32) * vn
    o_ref[...] = _rms_rows(o, gsub_ref[...]) * (1.0 - lam_init)


def _sample_attention(page_table, ck, cv, q, kn, vn, bm, lq1, lk1, lq2, lk2, gsub, *, layer, lam_init):
    nb, n_pages = page_table.shape
    pt_flat = page_table.reshape(nb * n_pages)
    page_spec = lambda p: pl.BlockSpec((None, None, PAGE, N_HEADS, V_DIM),
                                       lambda b, pt: (layer, pt[b * n_pages + p], 0, 0, 0))
    rowspec = pl.BlockSpec((None, N_HEADS, V_DIM), lambda b, pt: (b, 0, 0))
    vec = lambda n: pl.BlockSpec((1, n), lambda b, pt: (0, 0))
    in_specs = ([page_spec(p) for p in range(n_pages)] + [page_spec(p) for p in range(n_pages)]
                + [rowspec, rowspec, rowspec,
                   pl.BlockSpec((2 * N_HEADS, n_pages * PAGE_ROWS), lambda b, pt: (0, 0)),
                   vec(QK_HALF), vec(QK_HALF), vec(QK_HALF), vec(QK_HALF), vec(V_DIM)])
    grid_spec = pltpu.PrefetchScalarGridSpec(
        num_scalar_prefetch=1, grid=(nb,), in_specs=in_specs,
        out_specs=pl.BlockSpec((None, N_HEADS, V_DIM), lambda b, pt: (b, 0, 0)))
    return pl.pallas_call(
        functools.partial(_sample_attn_kernel, n_pages=n_pages, lam_init=lam_init),
        out_shape=jax.ShapeDtypeStruct((nb, N_HEADS, V_DIM), F32),
        grid_spec=grid_spec, compiler_params=_cparams(1, 48), name="sample_attention",
    )(pt_flat, *([ck] * n_pages), *([cv] * n_pages), q, kn, vn, bm, lq1, lk1, lq2, lk2, gsub)


INFO_E1, INFO_E2, INFO_W1, INFO_W2, INFO_R1, INFO_R2 = 0, 1, 2, 3, 4, 5


def _lane_min_index(mask, lane):
    return jnp.min(jnp.where(mask, lane, float(LANES)), axis=-1, keepdims=True)


def _out_proj_kernel(o_ref, yc_ref, x_ref, w_ref, gffn_ref, wrh_ref, wrl_ref, br_ref, tri_ref, cnt_in_ref,
                     xm_ref, h2_ref, info_ref, cnt_ref, cnt_scr, *, tm):
    i = pl.program_id(0)

    @pl.when(i == 0)
    def _():
        cnt_scr[...] = cnt_in_ref[...]

    xm = x_ref[...] + _dot(o_ref[...], w_ref[0:ATT_W, :]) + _dot(yc_ref[...], w_ref[ATT_W:D_MODEL, :])
    xm_ref[...] = xm
    h2 = _rms_rows(xm, gffn_ref[...])
    for s in range(N_LT):
        h2_ref[pl.ds(s, tm, stride=N_LT), :] = h2[:, s * LANES:(s + 1) * LANES]

    hh = h2.astype(BF16)
    hl = (h2 - hh.astype(F32)).astype(BF16)
    logits = _dot(hh, wrh_ref[...]) + _dot(hl, wrh_ref[...]) + _dot(hh, wrl_ref[...]) + br_ref[...]

    lane = lax.broadcasted_iota(jnp.int32, (tm, LANES), 1).astype(F32)
    is_grp = lane < N_GROUPS
    lg = jnp.where(is_grp, logits, NEG_BIG)
    mg = jnp.max(lg, axis=-1, keepdims=True)
    g_star = _lane_min_index(is_grp & (lg == mg), lane)
    p_group = 1.0 / jnp.sum(jnp.where(is_grp, jnp.exp(lg - mg), 0.0), axis=-1, keepdims=True)

    e_lo = N_GROUPS + g_star * EPG
    in_grp = (lane >= e_lo) & (lane < e_lo + EPG)
    le = jnp.where(in_grp, logits, NEG_BIG)
    m1 = jnp.max(le, axis=-1, keepdims=True)
    ex = jnp.where(in_grp, jnp.exp(le - m1), 0.0)
    pe = ex / jnp.sum(ex, axis=-1, keepdims=True)
    pe_m = jnp.where(in_grp, pe, -1.0)
    p1 = jnp.max(pe_m, axis=-1, keepdims=True)
    i1 = _lane_min_index(pe_m == p1, lane)
    pe_m2 = jnp.where(lane == i1, -1.0, pe_m)
    p2 = jnp.max(pe_m2, axis=-1, keepdims=True)
    i2 = _lane_min_index(pe_m2 == p2, lane)
    denom = p1 + p2
    w1 = p_group * p1 / denom
    w2 = p_group * p2 / denom
    e1 = i1 - N_GROUPS
    e2 = i2 - N_GROUPS

    oh1 = lane == e1
    oh2 = lane == e2
    oh = jnp.where(oh1 | oh2, 1.0, 0.0)
    prefix = _dot(tri_ref[...], oh.astype(BF16)) + cnt_scr[...]
    r1 = jnp.sum(jnp.where(oh1, prefix, 0.0), axis=-1, keepdims=True)
    r2 = jnp.sum(jnp.where(oh2, prefix, 0.0), axis=-1, keepdims=True)
    cnt_scr[...] = cnt_scr[...] + jnp.sum(oh, axis=0, keepdims=True)
    cnt_ref[...] = cnt_scr[...]

    info = jnp.zeros((tm, LANES), F32)
    for idx, val in ((INFO_E1, e1), (INFO_E2, e2), (INFO_W1, w1), (INFO_W2, w2),
                     (INFO_R1, r1), (INFO_R2, r2)):
        info = jnp.where(lane == idx, val, info)
    info_ref[...] = info


def _out_proj(o, yc, x, w_bf, gffn, wrh, wrl, br, tri, cnt_in, *, tm):
    t = x.shape[0]
    row = lambda i: (i, 0)
    in_specs = [
        pl.BlockSpec((tm, ATT_W), row), pl.BlockSpec((tm, CONV_W), row), pl.BlockSpec((tm, D_MODEL), row),
        _const_spec((D_MODEL, D_MODEL)), _const_spec((1, D_MODEL)),
        _const_spec((D_MODEL, LANES)), _const_spec((D_MODEL, LANES)), _const_spec((1, LANES)),
        _const_spec((tm, tm)), _const_spec((1, LANES)),
    ]
    out_specs = (
        pl.BlockSpec((tm, D_MODEL), row),
        pl.BlockSpec((tm * N_LT, LANES), row),
        pl.BlockSpec((tm, LANES), row),
        _const_spec((1, LANES)),
    )
    out_shape = (
        jax.ShapeDtypeStruct((t, D_MODEL), F32), jax.ShapeDtypeStruct((t * N_LT, LANES), F32),
        jax.ShapeDtypeStruct((t, LANES), F32), jax.ShapeDtypeStruct((1, LANES), F32),
    )
    return pl.pallas_call(
        functools.partial(_out_proj_kernel, tm=tm),
        out_shape=out_shape, grid=(t // tm,), in_specs=in_specs, out_specs=out_specs,
        scratch_shapes=[pltpu.VMEM((1, LANES), F32)],
        compiler_params=_cparams(1, 48), name="out_proj",
    )(o, yc, x, w_bf, gffn, wrh, wrl, br, tri, cnt_in)


def _row_copy(src, dst, sem):
    return pltpu.make_async_copy(src, dst, sem)


def _rows(ref, first, n=1):
    return ref.at[pl.ds(pl.multiple_of(first * N_LT, N_LT), n * N_LT)]


def _dispatch_kernel(d1_ref, d2_ref, s1_ref, s2_ref, fill_ref, end_ref, na_ref,
                     h2p_ref, h2s_ref, xb_hbm, zero_scr, sem, zsem, *, tm, n_sample, n_blocks):
    step = pl.program_id(0)

    @pl.when(step == 0)
    def _():
        zero_scr[...] = jnp.zeros(zero_scr.shape, F32)

        def pad_rows(fn):
            def per_expert(e, c):
                return lax.fori_loop(fill_ref[e], end_ref[e], lambda r, c2: fn(r, c2), c)
            lax.fori_loop(0, N_EXPERTS, per_expert, 0)

        def tail_blocks(fn):
            lax.fori_loop(na_ref[0], n_blocks, lambda b, c: fn(b, c), 0)

        def start_row(r, c):
            _row_copy(_rows(zero_scr, 0), _rows(xb_hbm, r), zsem).start()
            return c

        def wait_row(r, c):
            _row_copy(_rows(zero_scr, 0), _rows(xb_hbm, 0), zsem).wait()
            return c

        def start_blk(b, c):
            _row_copy(zero_scr, _rows(xb_hbm, b * MOE_BLK, MOE_BLK), zsem).start()
            return c

        def wait_blk(b, c):
            _row_copy(zero_scr, _rows(xb_hbm, 0, MOE_BLK), zsem).wait()
            return c

        pad_rows(start_row)
        tail_blocks(start_blk)

        def sample_row(r, c):
            _row_copy(_rows(h2s_ref, r), _rows(xb_hbm, s1_ref[r]), zsem).start()
            _row_copy(_rows(h2s_ref, r), _rows(xb_hbm, s2_ref[r]), zsem).start()
            return c

        lax.fori_loop(0, n_sample, sample_row, 0)
        pad_rows(wait_row)
        tail_blocks(wait_blk)
        _row_copy(h2s_ref, _rows(xb_hbm, 0, n_sample), zsem).wait()
        _row_copy(h2s_ref, _rows(xb_hbm, 0, n_sample), zsem).wait()

    base = step * tm

    def issue(r, c):
        t = base + r
        _row_copy(_rows(h2p_ref, r), _rows(xb_hbm, d1_ref[t]), sem).start()
        _row_copy(_rows(h2p_ref, r), _rows(xb_hbm, d2_ref[t]), sem).start()
        return c

    lax.fori_loop(0, tm, issue, 0)
    _row_copy(h2p_ref, _rows(xb_hbm, 0, tm), sem).wait()
    _row_copy(h2p_ref, _rows(xb_hbm, 0, tm), sem).wait()


def _dispatch(d1_p, d2_p, d1_s, d2_s, fill, end, n_active, h2_p, h2_s, *, n_blocks, tm):
    n_p = h2_p.shape[0] // N_LT
    n_s = h2_s.shape[0] // N_LT
    grid_spec = pltpu.PrefetchScalarGridSpec(
        num_scalar_prefetch=7, grid=(n_p // tm,),
        in_specs=[pl.BlockSpec((tm * N_LT, LANES), lambda i, *_: (i, 0)),
                  pl.BlockSpec((n_s * N_LT, LANES), lambda i, *_: (0, 0))],
        out_specs=pl.BlockSpec(memory_space=pl.ANY),
        scratch_shapes=[pltpu.VMEM((MOE_BLK * N_LT, LANES), F32),
                        pltpu.SemaphoreType.DMA(()), pltpu.SemaphoreType.DMA(())])
    return pl.pallas_call(
        functools.partial(_dispatch_kernel, tm=tm, n_sample=n_s, n_blocks=n_blocks),
        out_shape=jax.ShapeDtypeStruct((n_blocks * MOE_BLK * N_LT, LANES), F32), grid_spec=grid_spec,
        compiler_params=_cparams(1, 32), name="moe_dispatch",
    )(d1_p, d2_p, d1_s, d2_s, fill, end, n_active, h2_p, h2_s)


def _experts_kernel(be_ref, na_ref, xb_ref, wg_ref, wu_ref, wd_ref, yb_ref, x_scr):
    del be_ref

    @pl.when(pl.program_id(0) < na_ref[0])
    def _():
        for s in range(N_LT):
            x_scr[:, s * LANES:(s + 1) * LANES] = xb_ref[pl.ds(s, MOE_BLK, stride=N_LT), :]
        x = x_scr[...].astype(BF16)
        g = _dot(x, wg_ref[...])
        u = _dot(x, wu_ref[...])
        a = (g * (1.0 / (1.0 + jnp.exp(-g))) * u).astype(BF16)
        y = _dot(a, wd_ref[...])
        for s in range(N_LT):
            yb_ref[pl.ds(s, MOE_BLK, stride=N_LT), :] = y[:, s * LANES:(s + 1) * LANES]

    @pl.when(pl.program_id(0) >= na_ref[0])
    def _():
        yb_ref[...] = jnp.zeros(yb_ref.shape, F32)


def _experts(block_e, n_active, xb, wg, wu, wd):
    n_slots = xb.shape[0] // N_LT
    n_blocks = n_slots // MOE_BLK
    blk = lambda b, be, na: (jnp.minimum(b, na[0] - 1), 0)
    wsel = lambda b, be, na: (be[b], 0, 0)
    grid_spec = pltpu.PrefetchScalarGridSpec(
        num_scalar_prefetch=2, grid=(n_blocks,),
        in_specs=[
            pl.BlockSpec((MOE_BLK * N_LT, LANES), blk),
            pl.BlockSpec((None, D_MODEL, D_EXPERT), wsel),
            pl.BlockSpec((None, D_MODEL, D_EXPERT), wsel),
            pl.BlockSpec((None, D_EXPERT, D_MODEL), wsel),
        ],
        out_specs=pl.BlockSpec((MOE_BLK * N_LT, LANES), lambda b, be, na: (b, 0)),
        scratch_shapes=[pltpu.VMEM((MOE_BLK, D_MODEL), F32)])
    return pl.pallas_call(
        _experts_kernel, out_shape=jax.ShapeDtypeStruct(xb.shape, F32), grid_spec=grid_spec,
        compiler_params=_cparams(1, 40), name="moe_experts",
    )(block_e, n_active, xb, wg, wu, wd)


def _combine_kernel(d1_ref, d2_ref, xm_ref, info_ref, yb_hbm, out_ref, a_scr, b_scr, sems, *, tm, n_tiles):
    i = pl.program_id(0)

    def fetch(tile, slot):
        base = tile * tm

        def issue(r, c):
            t = base + r
            _row_copy(_rows(yb_hbm, d1_ref[t]), _rows(a_scr.at[slot], r), sems.at[slot]).start()
            _row_copy(_rows(yb_hbm, d2_ref[t]), _rows(b_scr.at[slot], r), sems.at[slot]).start()
            return c

        lax.fori_loop(0, tm, issue, 0)

    @pl.when(i == 0)
    def _():
        fetch(0, 0)

    @pl.when(i + 1 < n_tiles)
    def _():
        fetch(i + 1, (i + 1) % 2)

    slot = i % 2
    _row_copy(_rows(yb_hbm, 0, tm), a_scr.at[slot], sems.at[slot]).wait()
    _row_copy(_rows(yb_hbm, 0, tm), b_scr.at[slot], sems.at[slot]).wait()

    info = info_ref[...]
    w1 = info[:, INFO_W1:INFO_W1 + 1]
    w2 = info[:, INFO_W2:INFO_W2 + 1]
    for s in range(N_LT):
        cols = slice(s * LANES, (s + 1) * LANES)
        lane_tile = pl.ds(s, tm, stride=N_LT)
        out_ref[:, cols] = xm_ref[:, cols] + (a_scr[slot, lane_tile, :] * w1 + b_scr[slot, lane_tile, :] * w2)


def _combine(dest1, dest2, xm_buf, info_buf, yb, *, tm):
    n_rows = xm_buf.shape[0]
    n_tiles = n_rows // tm
    grid_spec = pltpu.PrefetchScalarGridSpec(
        num_scalar_prefetch=2, grid=(n_tiles,),
        in_specs=[
            pl.BlockSpec((tm, D_MODEL), lambda i, a, b: (i, 0)),
            pl.BlockSpec((tm, LANES), lambda i, a, b: (i, 0)),
            pl.BlockSpec(memory_space=pl.ANY),
        ],
        out_specs=pl.BlockSpec((tm, D_MODEL), lambda i, a, b: (i, 0)),
        scratch_shapes=[pltpu.VMEM((2, tm * N_LT, LANES), F32), pltpu.VMEM((2, tm * N_LT, LANES), F32),
                        pltpu.SemaphoreType.DMA((2,))])
    return pl.pallas_call(
        functools.partial(_combine_kernel, tm=tm, n_tiles=n_tiles),
        out_shape=jax.ShapeDtypeStruct((n_rows, D_MODEL), F32), grid_spec=grid_spec,
        compiler_params=_cparams(1, 32), name="moe_combine",
    )(dest1, dest2, xm_buf, info_buf, yb)


def _layer(l, xp, xs, cache_k, cache_v, state_conv, page_table, norm_mix, w_in, g_q, g_k,
           lam_q1, lam_k1, lam_q2, lam_k2, g_sub, conv_w, w_out, norm_ffn, w_rg, b_rg, w_re, b_re,
           w_gate, w_up, w_down):
    batch, seq, _ = xp.shape
    nb = xs.shape[0]
    n_pages = page_table.shape[1]
    past = n_pages * PAGE
    t_p = batch * seq
    n_tok = t_p + nb
    lam_init = 0.8 - 0.6 * math.exp(-0.3 * l)

    w_in_bf = w_in[l].astype(BF16)
    w_out_bf = w_out[l].astype(BF16)
    gmix = norm_mix[l].reshape(1, D_MODEL)
    gffn = norm_ffn[l].reshape(1, D_MODEL)
    gq = jnp.tile(g_q[l], ATT_W // QK_HALF).reshape(1, ATT_W)
    gk = jnp.tile(g_k[l], ATT_W // QK_HALF).reshape(1, ATT_W)
    gsub = g_sub[l].reshape(1, V_DIM)
    grp = jnp.arange(IN_CHUNK) // QK_HALF
    gsum = jnp.where(grp[:, None] == grp[None, :], 1.0 / QK_HALF, 0.0).astype(BF16)
    lq1, lk1, lq2, lk2 = (v[l].reshape(1, QK_HALF) for v in (lam_q1, lam_k1, lam_q2, lam_k2))
    slopes = 2.0 ** (-8.0 * jnp.arange(1, N_HEADS + 1, dtype=F32) / N_HEADS)
    lane = jnp.arange(LANES)
    qal = jnp.stack([jnp.where((lane == QK_HALF) | (lane == QK_HALF + 1), slopes[:, None], 0.0),
                     jnp.where((lane == 0) | (lane == 1), slopes[:, None], 0.0)], axis=1)
    w_r = jnp.zeros((D_MODEL, LANES), F32)
    w_r = w_r.at[:, 0:N_GROUPS].set(w_rg[l]).at[:, N_GROUPS:N_GROUPS + N_EXPERTS].set(w_re[l])
    wrh = w_r.astype(BF16)
    wrl = (w_r - wrh.astype(F32)).astype(BF16)
    br = jnp.zeros((1, LANES), F32)
    br = br.at[0, 0:N_GROUPS].set(b_rg[l]).at[0, N_GROUPS:N_GROUPS + N_EXPERTS].set(b_re[l])
    wg_bf, wu_bf, wd_bf = w_gate[l].astype(BF16), w_up[l].astype(BF16), w_down[l].astype(BF16)

    x2p = xp.reshape(t_p, D_MODEL)
    x2s = xs.reshape(nb, D_MODEL)
    q2, k2, vh, kf_p, vf_p, yc_p, cs_p = _in_proj_prompt(
        x2p, gmix, w_in_bf, gq, gk, gsum, conv_w[l], qal, batch=batch, seq=seq)
    o_p = _prompt_attention(q2, k2, vh, lq1, lk1, lq2, lk2, gsub, lam_init=lam_init)

    prev0 = state_conv[l, :, 0, :]
    prev1 = state_conv[l, :, 1, :]
    q_s, kf_s, vf_s, yc_s, u_s = _in_proj_sample(x2s, gmix, w_in_bf, gq, gk, gsum, conv_w[l], prev0, prev1)
    row_pos = jnp.arange(past * N_HEADS, dtype=jnp.int32) // N_HEADS
    row_head = jnp.arange(past * N_HEADS, dtype=jnp.int32) % N_HEADS
    q_head = jnp.arange(2 * N_HEADS, dtype=jnp.int32) % N_HEADS
    bias = -(slopes[q_head][:, None] * (past - row_pos).astype(F32)[None, :])
    bm = jnp.where(q_head[:, None] == row_head[None, :], bias, NEG_BIG)
    o_s = _sample_attention(page_table, cache_k, cache_v, q_s, kf_s, vf_s, bm, lq1, lk1, lq2, lk2, gsub,
                            layer=l, lam_init=lam_init)
    o_s = o_s.reshape(nb, ATT_W).astype(BF16)

    cnt0 = jnp.zeros((1, LANES), F32)
    tm_p, tm_s = 256, nb
    tri = lambda n: (jnp.arange(n)[:, None] > jnp.arange(n)[None, :]).astype(BF16)
    xm_p, h2_p, info_p, cnt1 = _out_proj(
        o_p.reshape(t_p, ATT_W), yc_p, x2p, w_out_bf, gffn, wrh, wrl, br, tri(tm_p), cnt0, tm=tm_p)
    xm_s, h2_s, info_s, cnt2 = _out_proj(
        o_s, yc_s, x2s, w_out_bf, gffn, wrh, wrl, br, tri(tm_s), cnt1, tm=tm_s)

    counts = cnt2[0, 0:N_EXPERTS].astype(jnp.int32)
    padded = (counts + MOE_BLK - 1) // MOE_BLK * MOE_BLK
    pad_end = jnp.cumsum(padded)
    pad_start = pad_end - padded
    n_blocks = -(-2 * n_tok // MOE_BLK) + N_EXPERTS
    blk_row = jnp.arange(n_blocks, dtype=jnp.int32) * MOE_BLK
    block_e = jnp.minimum(jnp.sum((pad_end[None, :] <= blk_row[:, None]).astype(jnp.int32), axis=1),
                          N_EXPERTS - 1)
    n_active = (pad_end[-1:] // MOE_BLK).astype(jnp.int32)
    experts = jnp.arange(N_EXPERTS, dtype=jnp.int32)

    def slots(info):
        def one(e_col, r_col):
            e = info[:, e_col].astype(jnp.int32)
            start = jnp.sum(jnp.where(e[:, None] == experts[None, :], pad_start[None, :], 0), axis=1)
            return start + info[:, r_col].astype(jnp.int32)
        return one(INFO_E1, INFO_R1), one(INFO_E2, INFO_R2)

    d1_p, d2_p = slots(info_p)
    d1_s, d2_s = slots(info_s)

    xb = _dispatch(d1_p, d2_p, d1_s, d2_s, pad_start + counts, pad_end, n_active, h2_p, h2_s,
                   n_blocks=n_blocks, tm=512)
    yb = _experts(block_e, n_active, xb, wg_bf, wu_bf, wd_bf)
    y_p = _combine(d1_p, d2_p, xm_p, info_p, yb, tm=128)
    y_s = _combine(d1_s, d2_s, xm_s, info_s, yb, tm=128)

    conv_s = jnp.stack([prev1, u_s], axis=1)
    return (y_p.reshape(batch, seq, D_MODEL), y_s.reshape(nb, 1, D_MODEL),
            kf_p.reshape(batch, seq, N_HEADS, V_DIM), vf_p.reshape(batch, seq, N_HEADS, V_DIM), cs_p,
            kf_s.reshape(nb, 1, N_HEADS, V_DIM), vf_s.reshape(nb, 1, N_HEADS, V_DIM), conv_s)


def kernel(x_prompt, x_sample, cache_k, cache_v, state_conv, page_table, norm_mix, w_in, g_q, g_k,
           lam_q1, lam_k1, lam_q2, lam_k2, g_sub, conv_w, w_out, norm_ffn, w_router_group, b_router_group,
           w_router_expert, b_router_expert, w_gate, w_up, w_down):
    depth = w_in.shape[0]
    xp, xs = x_prompt, x_sample
    outs = [[] for _ in range(6)]
    for l in range(depth):
        xp, xs, *rest = _layer(
            l, xp, xs, cache_k, cache_v, state_conv, page_table, norm_mix, w_in, g_q, g_k,
            lam_q1, lam_k1, lam_q2, lam_k2, g_sub, conv_w, w_out, norm_ffn, w_router_group, b_router_group,
            w_router_expert, b_router_expert, w_gate, w_up, w_down)
        for acc, val in zip(outs, rest):
            acc.append(val)
    return (xp, xs) + tuple(jnp.stack(v, axis=0) for v in outs)
```

```python
import functools
import math

import jax
import jax.numpy as jnp
from jax import lax
from jax.experimental import pallas as pl
from jax.experimental.pallas import tpu as pltpu

D_MODEL = 2048
N_HEADS = 8
V_DIM = 128
QK_HALF = 64
ATT_W = N_HEADS * V_DIM
CONV_W = D_MODEL - ATT_W
CONV_K = 3
IN_COLS = 3 * ATT_W + 3 * CONV_W
N_GROUPS = 4
EPG = 8
N_EXPERTS = N_GROUPS * EPG
D_EXPERT = D_MODEL // 4
PAGE = 128
EPS = 1e-6

LANES = 128
SUBLANES = 8
N_LT = D_MODEL // LANES
MOE_BLK = 128
NEG_BIG = -1e30
POS_SPLIT = 16
ALIBI_LANES = 6
LOG2E = math.log2(math.e)
PAGE_ROWS = PAGE * N_HEADS

F32 = jnp.float32
BF16 = jnp.bfloat16


def _cparams(n_axes, vmem_mb):
    return pltpu.CompilerParams(
        dimension_semantics=("arbitrary",) * n_axes,
        vmem_limit_bytes=vmem_mb * 1024 * 1024)


def _const_spec(shape, single_buffer=False):
    nd = len(shape)
    idx = lambda *_: (0,) * nd
    if single_buffer:
        return pl.BlockSpec(shape, idx, pipeline_mode=pl.Buffered(1))
    return pl.BlockSpec(shape, idx)


def _dot(a, b):
    return jnp.dot(a, b, preferred_element_type=F32)


def _dot_nt(a, b):
    return lax.dot_general(a, b, (((1,), (1,)), ((), ())), preferred_element_type=F32)


def _rms_rows(x, gain):
    ms = jnp.mean(x * x, axis=-1, keepdims=True)
    return x * lax.rsqrt(ms + EPS) * gain


def _diff_lambda(lq1, lk1, lq2, lk2, lam_init):
    s1 = jnp.sum(lq1 * lk1, axis=-1, keepdims=True)
    s2 = jnp.sum(lq2 * lk2, axis=-1, keepdims=True)
    return jnp.exp(s1) - jnp.exp(s2) + lam_init


IN_CHUNK = 512


def _half_norm(p, gsum, gain):
    ms = _dot((p * p).astype(BF16), gsum)
    return p * lax.rsqrt(ms + EPS) * gain


def _in_proj_prompt_kernel(x_ref, gmix_ref, w_ref, gq_ref, gk_ref, gsum_ref, cw_ref, qal_ref,
                           q2_ref, k2_ref, vh_ref, kf_ref, vf_ref, yc_ref, cs_ref,
                           h_scr, u_scr, *, tm, tiles_per_seq):
    i = pl.program_id(0)

    @pl.when(i % tiles_per_seq == 0)
    def _():
        u_scr[0:SUBLANES, :] = jnp.zeros((SUBLANES, CONV_W), F32)

    h_scr[...] = _rms_rows(x_ref[...], gmix_ref[...]).astype(BF16)
    gsum = gsum_ref[...]
    lane = lax.broadcasted_iota(jnp.int32, (tm, LANES), 1)
    first_half = lane < QK_HALF
    heads_per_chunk = IN_CHUNK // V_DIM

    pos = lax.broadcasted_iota(jnp.int32, (tm, LANES), 0) + (i % tiles_per_seq) * tm
    pos_hi = (pos // POS_SPLIT * POS_SPLIT).astype(F32)
    pos_lo = (pos % POS_SPLIT).astype(F32)

    def key_alibi(first_lane):
        rel = lane - first_lane
        return jnp.where((rel >= 0) & (rel < ALIBI_LANES), jnp.where(rel % 2 == 0, pos_hi, pos_lo), 0.0)

    kal_a = key_alibi(QK_HALF)
    kal_b = key_alibi(0)

    for c in range(ATT_W // IN_CHUNK):
        cols = slice(c * IN_CHUNK, (c + 1) * IN_CHUNK)
        p = _dot(h_scr[...], w_ref[:, c * IN_CHUNK:(c + 1) * IN_CHUNK])
        qn = _half_norm(p, gsum, gq_ref[:, cols]) * (LOG2E * QK_HALF ** -0.5)
        for hh in range(heads_per_chunk):
            qh = qn[:, hh * V_DIM:(hh + 1) * V_DIM]
            head = c * heads_per_chunk + hh
            q2_ref[head, 0] = jnp.where(first_half, qh, qal_ref[head, 0:1, :]).astype(BF16)
            q2_ref[head, 1] = jnp.where(first_half, qal_ref[head, 1:2, :], qh).astype(BF16)
        p = _dot(h_scr[...], w_ref[:, ATT_W + c * IN_CHUNK:ATT_W + (c + 1) * IN_CHUNK])
        kn = _half_norm(p, gsum, gk_ref[:, cols])
        kf_ref[:, cols] = kn
        for hh in range(heads_per_chunk):
            kh = kn[:, hh * V_DIM:(hh + 1) * V_DIM]
            head = c * heads_per_chunk + hh
            k2_ref[head, 0] = jnp.where(first_half, kh, kal_a).astype(BF16)
            k2_ref[head, 1] = jnp.where(first_half, kal_b, kh).astype(BF16)
        p = _dot(h_scr[...], w_ref[:, 2 * ATT_W + c * IN_CHUNK:2 * ATT_W + (c + 1) * IN_CHUNK])
        vf_ref[:, cols] = p
        for hh in range(heads_per_chunk):
            vh_ref[c * heads_per_chunk + hh] = p[:, hh * V_DIM:(hh + 1) * V_DIM].astype(BF16)

    base = 3 * ATT_W
    for c in range(CONV_W // IN_CHUNK):
        cols = slice(c * IN_CHUNK, (c + 1) * IN_CHUNK)
        lo = c * IN_CHUNK
        bg = _dot(h_scr[...], w_ref[:, base + lo:base + lo + IN_CHUNK])
        cg = _dot(h_scr[...], w_ref[:, base + CONV_W + lo:base + CONV_W + lo + IN_CHUNK])
        xc = _dot(h_scr[...], w_ref[:, base + 2 * CONV_W + lo:base + 2 * CONV_W + lo + IN_CHUNK])
        u = cg * xc
        u_scr[SUBLANES:SUBLANES + tm, cols] = u
        u1 = u_scr[SUBLANES - 1:SUBLANES - 1 + tm, cols]
        u2 = u_scr[SUBLANES - 2:SUBLANES - 2 + tm, cols]
        y = cw_ref[0:1, cols] * u2 + cw_ref[1:2, cols] * u1 + cw_ref[2:3, cols] * u
        yc_ref[:, cols] = (bg * y).astype(BF16)

    cs_ref[...] = u_scr[tm + SUBLANES - 2:tm + SUBLANES, :]
    u_scr[0:SUBLANES, :] = u_scr[tm:tm + SUBLANES, :]


def _in_proj_sample_kernel(x_ref, gmix_ref, w_ref, gq_ref, gk_ref, gsum_ref, cw_ref, p0_ref, p1_ref,
                           q_ref, kf_ref, vf_ref, yc_ref, u_ref, h_scr):
    h_scr[...] = _rms_rows(x_ref[...], gmix_ref[...]).astype(BF16)
    gsum = gsum_ref[...]
    heads_per_chunk = IN_CHUNK // V_DIM
    for c in range(ATT_W // IN_CHUNK):
        cols = slice(c * IN_CHUNK, (c + 1) * IN_CHUNK)
        p = _dot(h_scr[...], w_ref[:, c * IN_CHUNK:(c + 1) * IN_CHUNK])
        qn = _half_norm(p, gsum, gq_ref[:, cols]) * (QK_HALF ** -0.5)
        p = _dot(h_scr[...], w_ref[:, ATT_W + c * IN_CHUNK:ATT_W + (c + 1) * IN_CHUNK])
        kn = _half_norm(p, gsum, gk_ref[:, cols])
        vn = _dot(h_scr[...], w_ref[:, 2 * ATT_W + c * IN_CHUNK:2 * ATT_W + (c + 1) * IN_CHUNK])
        for hh in range(heads_per_chunk):
            head = c * heads_per_chunk + hh
            hcols = slice(hh * V_DIM, (hh + 1) * V_DIM)
            q_ref[:, head, :] = qn[:, hcols]
            kf_ref[:, head, :] = kn[:, hcols]
            vf_ref[:, head, :] = vn[:, hcols]
    base = 3 * ATT_W
    for c in range(CONV_W // IN_CHUNK):
        cols = slice(c * IN_CHUNK, (c + 1) * IN_CHUNK)
        lo = c * IN_CHUNK
        bg = _dot(h_scr[...], w_ref[:, base + lo:base + lo + IN_CHUNK])
        cg = _dot(h_scr[...], w_ref[:, base + CONV_W + lo:base + CONV_W + lo + IN_CHUNK])
        xc = _dot(h_scr[...], w_ref[:, base + 2 * CONV_W + lo:base + 2 * CONV_W + lo + IN_CHUNK])
        u = cg * xc
        u_ref[:, cols] = u
        y = cw_ref[0:1, cols] * p0_ref[:, cols] + cw_ref[1:2, cols] * p1_ref[:, cols] + cw_ref[2:3, cols] * u
        yc_ref[:, cols] = (bg * y).astype(BF16)


def _in_proj_prompt(x, gmix, w_bf, gq, gk, gsum, cw, qal, *, batch, seq, tm=256):
    t = batch * seq
    tps = seq // tm
    row = lambda i: (i, 0)
    out_shape = (
        jax.ShapeDtypeStruct((batch, N_HEADS, 2, seq, V_DIM), BF16),
        jax.ShapeDtypeStruct((batch, N_HEADS, 2, seq, V_DIM), BF16),
        jax.ShapeDtypeStruct((batch, N_HEADS, seq, V_DIM), BF16),
        jax.ShapeDtypeStruct((t, ATT_W), F32),
        jax.ShapeDtypeStruct((t, ATT_W), F32),
        jax.ShapeDtypeStruct((t, CONV_W), BF16),
        jax.ShapeDtypeStruct((batch, CONV_K - 1, CONV_W), F32),
    )
    out_specs = (
        pl.BlockSpec((None, N_HEADS, 2, tm, V_DIM), lambda i: (i // tps, 0, 0, i % tps, 0)),
        pl.BlockSpec((None, N_HEADS, 2, tm, V_DIM), lambda i: (i // tps, 0, 0, i % tps, 0)),
        pl.BlockSpec((None, N_HEADS, tm, V_DIM), lambda i: (i // tps, 0, i % tps, 0)),
        pl.BlockSpec((tm, ATT_W), row),
        pl.BlockSpec((tm, ATT_W), row),
        pl.BlockSpec((tm, CONV_W), row),
        pl.BlockSpec((None, CONV_K - 1, CONV_W), lambda i: (i // tps, 0, 0)),
    )
    in_specs = [
        pl.BlockSpec((tm, D_MODEL), row),
        _const_spec((1, D_MODEL)),
        _const_spec((D_MODEL, IN_COLS), single_buffer=True),
        _const_spec((1, ATT_W)),
        _const_spec((1, ATT_W)),
        _const_spec((IN_CHUNK, IN_CHUNK)),
        _const_spec((CONV_K, CONV_W)),
        _const_spec((N_HEADS, 2, LANES)),
    ]
    return pl.pallas_call(
        functools.partial(_in_proj_prompt_kernel, tm=tm, tiles_per_seq=tps),
        out_shape=out_shape, grid=(t // tm,), in_specs=in_specs, out_specs=out_specs,
        scratch_shapes=[pltpu.VMEM((tm, D_MODEL), BF16), pltpu.VMEM((tm + SUBLANES, CONV_W), F32)],
        compiler_params=_cparams(1, 56), name="in_proj_prompt",
    )(x, gmix, w_bf, gq, gk, gsum, cw, qal)


def _in_proj_sample(x, gmix, w_bf, gq, gk, gsum, cw, prev0, prev1):
    t = x.shape[0]
    full = lambda shape: _const_spec(shape)
    out_shape = (
        jax.ShapeDtypeStruct((t, N_HEADS, V_DIM), F32),
        jax.ShapeDtypeStruct((t, N_HEADS, V_DIM), F32),
        jax.ShapeDtypeStruct((t, N_HEADS, V_DIM), F32),
        jax.ShapeDtypeStruct((t, CONV_W), BF16),
        jax.ShapeDtypeStruct((t, CONV_W), F32),
    )
    in_specs = [
        full((t, D_MODEL)), full((1, D_MODEL)), _const_spec((D_MODEL, IN_COLS), single_buffer=True),
        full((1, ATT_W)), full((1, ATT_W)), full((IN_CHUNK, IN_CHUNK)), full((CONV_K, CONV_W)),
        full((t, CONV_W)), full((t, CONV_W)),
    ]
    return pl.pallas_call(
        _in_proj_sample_kernel, out_shape=out_shape, grid=(1,), in_specs=in_specs,
        out_specs=tuple(full(s.shape) for s in out_shape),
        scratch_shapes=[pltpu.VMEM((t, D_MODEL), BF16)],
        compiler_params=_cparams(1, 48), name="in_proj_sample",
    )(x, gmix, w_bf, gq, gk, gsum, cw, prev0, prev1)


def _prompt_attn_kernel(q2_ref, k2_ref, v_ref, lq1_ref, lk1_ref, lq2_ref, lk2_ref, gsub_ref,
                        o_ref, s_scr, v2_scr, acc_scr, *, tq, lam_init):
    i = pl.program_id(2)
    n_lt = tq // LANES
    row = lax.broadcasted_iota(jnp.int32, (tq, tq), 0)
    col = lax.broadcasted_iota(jnp.int32, (tq, tq), 1)
    causal = col <= row

    def lane_tile_max(m_acc, s):
        for t in range(n_lt):
            m_acc = jnp.maximum(m_acc, s[:, t * LANES:(t + 1) * LANES])
        return m_acc

    def lane_tile_sum(l_acc, p):
        for t in range(n_lt):
            l_acc = l_acc + p[:, t * LANES:(t + 1) * LANES]
        return l_acc

    def scores(j, c):
        start = pl.multiple_of(j * tq, tq)
        return _dot_nt(q2_ref[c], k2_ref[c, pl.ds(start, tq), :])

    def pass1(j, carry):
        out = []
        for c in range(2):
            s = scores(j, c)
            s_scr[c, j] = s
            out.append(lane_tile_max(carry[c], s))
        return tuple(out)

    neg = jnp.full((tq, LANES), NEG_BIG, F32)
    m_acc = lax.fori_loop(0, i, pass1, (neg, neg))
    m_rows = []
    for c in range(2):
        s = jnp.where(causal, scores(i, c), NEG_BIG)
        s_scr[c, i] = s
        m_rows.append(jnp.max(lane_tile_max(m_acc[c], s), axis=-1, keepdims=True))

    @pl.when(i == 0)
    def _():
        v2_scr[...] = jnp.zeros(v2_scr.shape, BF16)
        for j in range(v2_scr.shape[0]):
            v = v_ref[j * tq:(j + 1) * tq, :]
            v2_scr[j, 0:tq, 0:V_DIM] = v
            v2_scr[j, tq:2 * tq, V_DIM:2 * V_DIM] = v

    acc_scr[...] = jnp.zeros(acc_scr.shape, F32)

    def pass2(j, carry):
        l0, l1 = carry
        p0 = jnp.exp2(s_scr[0, j] - m_rows[0])
        p1 = jnp.exp2(s_scr[1, j] - m_rows[1])
        pcat = jnp.concatenate([p0.astype(BF16), p1.astype(BF16)], axis=1)
        acc_scr[...] += _dot(pcat, v2_scr[j])
        return lane_tile_sum(l0, p0), lane_tile_sum(l1, p1)

    z = jnp.zeros((tq, LANES), F32)
    l0, l1 = lax.fori_loop(0, i + 1, pass2, (z, z))
    a0 = acc_scr[:, 0:V_DIM]
    a1 = acc_scr[:, V_DIM:2 * V_DIM]
    lam = _diff_lambda(lq1_ref[...], lk1_ref[...], lq2_ref[...], lk2_ref[...], lam_init)
    o = a0 / jnp.sum(l0, axis=-1, keepdims=True) - lam * (a1 / jnp.sum(l1, axis=-1, keepdims=True))
    o_ref[...] = (_rms_rows(o, gsub_ref[...]) * (1.0 - lam_init)).astype(BF16)


def _prompt_attention(q2, k2, vh, lq1, lk1, lq2, lk2, gsub, *, lam_init, tq=512):
    batch, _, _, seq, _ = q2.shape
    nq = seq // tq
    vec = lambda n: pl.BlockSpec((1, n), lambda b, h, i: (0, 0))
    return pl.pallas_call(
        functools.partial(_prompt_attn_kernel, tq=tq, lam_init=lam_init),
        out_shape=jax.ShapeDtypeStruct((batch, seq, ATT_W), BF16),
        grid=(batch, N_HEADS, nq),
        in_specs=[
            pl.BlockSpec((None, None, 2, tq, V_DIM), lambda b, h, i: (b, h, 0, i, 0)),
            pl.BlockSpec((None, None, 2, seq, V_DIM), lambda b, h, i: (b, h, 0, 0, 0)),
            pl.BlockSpec((None, None, seq, V_DIM), lambda b, h, i: (b, h, 0, 0)),
            vec(QK_HALF), vec(QK_HALF), vec(QK_HALF), vec(QK_HALF), vec(V_DIM),
        ],
        out_specs=pl.BlockSpec((None, tq, V_DIM), lambda b, h, i: (b, i, h)),
        scratch_shapes=[pltpu.VMEM((2, nq, tq, tq), F32), pltpu.VMEM((nq, 2 * tq, 2 * V_DIM), BF16),
                        pltpu.VMEM((tq, 2 * V_DIM), F32)],
        compiler_params=_cparams(3, 40), name="prompt_attention",
    )(q2, k2, vh, lq1, lk1, lq2, lk2, gsub)


def _sample_attn_kernel(pt_ref, *refs, n_pages, lam_init):
    del pt_ref
    k_refs = refs[:n_pages]
    v_refs = refs[n_pages:2 * n_pages]
    (q_ref, kn_ref, vn_ref, bm_ref, lq1_ref, lk1_ref, lq2_ref, lk2_ref, gsub_ref, o_ref) = refs[2 * n_pages:]
    n_rows = 2 * N_HEADS

    q8 = q_ref[...]
    lane = lax.broadcasted_iota(jnp.int32, (N_HEADS, V_DIM), 1)
    qm = jnp.concatenate([jnp.where(lane < QK_HALF, q8, 0.0), jnp.where(lane < QK_HALF, 0.0, q8)], axis=0)
    qm_bf = qm.astype(BF16)

    s_parts = [_dot_nt(qm_bf, k_refs[p][...].reshape(PAGE_ROWS, V_DIM).astype(BF16)) for p in range(n_pages)]
    s = jnp.concatenate(s_parts, axis=-1) + bm_ref[...]
    kn = kn_ref[...].astype(BF16).astype(F32)
    s_new = jnp.sum(qm_bf.astype(F32) * jnp.concatenate([kn, kn], axis=0), axis=-1, keepdims=True)

    m = jnp.maximum(jnp.max(s, axis=-1, keepdims=True), s_new)
    p = jnp.exp(s - m)
    p_new = jnp.exp(s_new - m)
    inv_l = 1.0 / (jnp.sum(p, axis=-1, keepdims=True) + p_new)
    lam = _diff_lambda(lq1_ref[...], lk1_ref[...], lq2_ref[...], lk2_ref[...], lam_init)
    w0 = inv_l[0:N_HEADS]
    w1 = lam * inv_l[N_HEADS:n_rows]
    a = p[0:N_HEADS] * w0 - p[N_HEADS:n_rows] * w1
    a_new = p_new[0:N_HEADS] * w0 - p_new[N_HEADS:n_rows] * w1
    a_bf = jnp.concatenate([a, jnp.zeros_like(a)], axis=0).astype(BF16)

    acc = jnp.zeros((n_rows, V_DIM), F32)
    for pg in range(n_pages):
        acc = acc + _dot(a_bf[:, pg * PAGE_ROWS:(pg + 1) * PAGE_ROWS],
                         v_refs[pg][...].reshape(PAGE_ROWS, V_DIM).astype(BF16))
    vn = vn_ref[...].astype(BF16).astype(F32)
    o = acc[0:N_HEADS] + a_new.astype(BF16).astype(F32) * vn
    o_ref[...] = _rms_rows(o, gsub_ref[...]) * (1.0 - lam_init)


def _sample_attention(page_table, ck, cv, q, kn, vn, bm, lq1, lk1, lq2, lk2, gsub, *, layer, lam_init):
    nb, n_pages = page_table.shape
    pt_flat = page_table.reshape(nb * n_pages)
    page_spec = lambda p: pl.BlockSpec((None, None, PAGE, N_HEADS, V_DIM),
                                       lambda b, pt: (layer, pt[b * n_pages + p], 0, 0, 0))
    rowspec = pl.BlockSpec((None, N_HEADS, V_DIM), lambda b, pt: (b, 0, 0))
    vec = lambda n: pl.BlockSpec((1, n), lambda b, pt: (0, 0))
    in_specs = ([page_spec(p) for p in range(n_pages)] + [page_spec(p) for p in range(n_pages)]
                + [rowspec, rowspec, rowspec,
                   pl.BlockSpec((2 * N_HEADS, n_pages * PAGE_ROWS), lambda b, pt: (0, 0)),
                   vec(QK_HALF), vec(QK_HALF), vec(QK_HALF), vec(QK_HALF), vec(V_DIM)])
    grid_spec = pltpu.PrefetchScalarGridSpec(
        num_scalar_prefetch=1, grid=(nb,), in_specs=in_specs,
        out_specs=pl.BlockSpec((None, N_HEADS, V_DIM), lambda b, pt: (b, 0, 0)))
    return pl.pallas_call(
        functools.partial(_sample_attn_kernel, n_pages=n_pages, lam_init=lam_init),
        out_shape=jax.ShapeDtypeStruct((nb, N_HEADS, V_DIM), F32),
        grid_spec=grid_spec, compiler_params=_cparams(1, 48), name="sample_attention",
    )(pt_flat, *([ck] * n_pages), *([cv] * n_pages), q, kn, vn, bm, lq1, lk1, lq2, lk2, gsub)


INFO_E1, INFO_E2, INFO_W1, INFO_W2, INFO_R1, INFO_R2 = 0, 1, 2, 3, 4, 5


def _lane_min_index(mask, lane):
    return jnp.min(jnp.where(mask, lane, float(LANES)), axis=-1, keepdims=True)


def _out_proj_kernel(o_ref, yc_ref, x_ref, w_ref, gffn_ref, wr2_ref, br_ref, tri_ref, cnt_in_ref,
                     xm_ref, h2_ref, info_ref, cnt_ref, cnt_scr, *, tm):
    i = pl.program_id(0)

    @pl.when(i == 0)
    def _():
        cnt_scr[...] = cnt_in_ref[...]

    xm = x_ref[...] + _dot(o_ref[...], w_ref[0:ATT_W, :]) + _dot(yc_ref[...], w_ref[ATT_W:D_MODEL, :])
    xm_ref[...] = xm
    h2 = _rms_rows(xm, gffn_ref[...])
    for s in range(N_LT):
        h2_ref[pl.ds(s, tm, stride=N_LT), :] = h2[:, s * LANES:(s + 1) * LANES]

    hh = h2.astype(BF16)
    hl = (h2 - hh.astype(F32)).astype(BF16)
    parts = _dot(jnp.concatenate([hh, hl], axis=0), wr2_ref[...])
    logits = (parts[0:tm, 0:LANES] + parts[0:tm, LANES:2 * LANES]
              + parts[tm:2 * tm, 0:LANES] + parts[tm:2 * tm, LANES:2 * LANES]) + br_ref[...]

    lane = lax.broadcasted_iota(jnp.int32, (tm, LANES), 1).astype(F32)
    is_grp = lane < N_GROUPS
    lg = jnp.where(is_grp, logits, NEG_BIG)
    mg = jnp.max(lg, axis=-1, keepdims=True)
    g_star = _lane_min_index(is_grp & (lg == mg), lane)
    p_group = 1.0 / jnp.sum(jnp.where(is_grp, jnp.exp(lg - mg), 0.0), axis=-1, keepdims=True)

    e_lo = N_GROUPS + g_star * EPG
    in_grp = (lane >= e_lo) & (lane < e_lo + EPG)
    le = jnp.where(in_grp, logits, NEG_BIG)
    m1 = jnp.max(le, axis=-1, keepdims=True)
    ex = jnp.where(in_grp, jnp.exp(le - m1), 0.0)
    pe = ex / jnp.sum(ex, axis=-1, keepdims=True)
    pe_m = jnp.where(in_grp, pe, -1.0)
    p1 = jnp.max(pe_m, axis=-1, keepdims=True)
    i1 = _lane_min_index(pe_m == p1, lane)
    pe_m2 = jnp.where(lane == i1, -1.0, pe_m)
    p2 = jnp.max(pe_m2, axis=-1, keepdims=True)
    i2 = _lane_min_index(pe_m2 == p2, lane)
    denom = p1 + p2
    w1 = p_group * p1 / denom
    w2 = p_group * p2 / denom
    e1 = i1 - N_GROUPS
    e2 = i2 - N_GROUPS

    oh1 = lane == e1
    oh2 = lane == e2
    oh = jnp.where(oh1 | oh2, 1.0, 0.0)
    prefix = _dot(tri_ref[...], oh.astype(BF16)) + cnt_scr[...]
    r1 = jnp.sum(jnp.where(oh1, prefix, 0.0), axis=-1, keepdims=True)
    r2 = jnp.sum(jnp.where(oh2, prefix, 0.0), axis=-1, keepdims=True)
    cnt_scr[...] = cnt_scr[...] + jnp.sum(oh, axis=0, keepdims=True)
    cnt_ref[...] = cnt_scr[...]

    info = jnp.zeros((tm, LANES), F32)
    for idx, val in ((INFO_E1, e1), (INFO_E2, e2), (INFO_W1, w1), (INFO_W2, w2),
                     (INFO_R1, r1), (INFO_R2, r2)):
        info = jnp.where(lane == idx, val, info)
    info_ref[...] = info


def _out_proj(o, yc, x, w_bf, gffn, wr2, br, tri, cnt_in, *, tm):
    t = x.shape[0]
    row = lambda i: (i, 0)
    in_specs = [
        pl.BlockSpec((tm, ATT_W), row), pl.BlockSpec((tm, CONV_W), row), pl.BlockSpec((tm, D_MODEL), row),
        _const_spec((D_MODEL, D_MODEL)), _const_spec((1, D_MODEL)),
        _const_spec((D_MODEL, 2 * LANES)), _const_spec((1, LANES)),
        _const_spec((tm, tm)), _const_spec((1, LANES)),
    ]
    out_specs = (
        pl.BlockSpec((tm, D_MODEL), row),
        pl.BlockSpec((tm * N_LT, LANES), row),
        pl.BlockSpec((tm, LANES), row),
        _const_spec((1, LANES)),
    )
    out_shape = (
        jax.ShapeDtypeStruct((t, D_MODEL), F32), jax.ShapeDtypeStruct((t * N_LT, LANES), F32),
        jax.ShapeDtypeStruct((t, LANES), F32), jax.ShapeDtypeStruct((1, LANES), F32),
    )
    return pl.pallas_call(
        functools.partial(_out_proj_kernel, tm=tm),
        out_shape=out_shape, grid=(t // tm,), in_specs=in_specs, out_specs=out_specs,
        scratch_shapes=[pltpu.VMEM((1, LANES), F32)],
        compiler_params=_cparams(1, 48), name="out_proj",
    )(o, yc, x, w_bf, gffn, wr2, br, tri, cnt_in)


def _row_copy(src, dst, sem):
    return pltpu.make_async_copy(src, dst, sem)


ISSUE_UNROLL = 8


def _for_each_row(n, fn):
    def chunk(k, c):
        for u in range(ISSUE_UNROLL):
            fn(k * ISSUE_UNROLL + u)
        return c
    lax.fori_loop(0, n // ISSUE_UNROLL, chunk, 0)


def _rows(ref, first, n=1):
    return ref.at[pl.ds(pl.multiple_of(first * N_LT, N_LT), n * N_LT)]


def _dispatch_kernel(d1_ref, d2_ref, s1_ref, s2_ref, fill_ref, end_ref, na_ref,
                     h2p_ref, h2s_ref, xb_hbm, zero_scr, sem, zsem, *, tm, n_sample, n_blocks):
    step = pl.program_id(0)

    @pl.when(step == 0)
    def _():
        zero_scr[...] = jnp.zeros(zero_scr.shape, F32)

        def pad_rows(fn):
            def per_expert(e, c):
                return lax.fori_loop(fill_ref[e], end_ref[e], lambda r, c2: fn(r, c2), c)
            lax.fori_loop(0, N_EXPERTS, per_expert, 0)

        def tail_blocks(fn):
            lax.fori_loop(na_ref[0], n_blocks, lambda b, c: fn(b, c), 0)

        def start_row(r, c):
            _row_copy(_rows(zero_scr, 0), _rows(xb_hbm, r), zsem).start()
            return c

        def wait_row(r, c):
            _row_copy(_rows(zero_scr, 0), _rows(xb_hbm, 0), zsem).wait()
            return c

        def start_blk(b, c):
            _row_copy(zero_scr, _rows(xb_hbm, b * MOE_BLK, MOE_BLK), zsem).start()
            return c

        def wait_blk(b, c):
            _row_copy(zero_scr, _rows(xb_hbm, 0, MOE_BLK), zsem).wait()
            return c

        pad_rows(start_row)
        tail_blocks(start_blk)

        def sample_row(r, c):
            _row_copy(_rows(h2s_ref, r), _rows(xb_hbm, s1_ref[r]), zsem).start()
            _row_copy(_rows(h2s_ref, r), _rows(xb_hbm, s2_ref[r]), zsem).start()
            return c

        lax.fori_loop(0, n_sample, sample_row, 0)
        pad_rows(wait_row)
        tail_blocks(wait_blk)
        _row_copy(h2s_ref, _rows(xb_hbm, 0, n_sample), zsem).wait()
        _row_copy(h2s_ref, _rows(xb_hbm, 0, n_sample), zsem).wait()

    base = step * tm

    def issue(r):
        t = base + r
        _row_copy(_rows(h2p_ref, r), _rows(xb_hbm, d1_ref[t]), sem).start()
        _row_copy(_rows(h2p_ref, r), _rows(xb_hbm, d2_ref[t]), sem).start()

    _for_each_row(tm, issue)
    _row_copy(h2p_ref, _rows(xb_hbm, 0, tm), sem).wait()
    _row_copy(h2p_ref, _rows(xb_hbm, 0, tm), sem).wait()


def _dispatch(d1_p, d2_p, d1_s, d2_s, fill, end, n_active, h2_p, h2_s, *, n_blocks, tm):
    n_p = h2_p.shape[0] // N_LT
    n_s = h2_s.shape[0] // N_LT
    grid_spec = pltpu.PrefetchScalarGridSpec(
        num_scalar_prefetch=7, grid=(n_p // tm,),
        in_specs=[pl.BlockSpec((tm * N_LT, LANES), lambda i, *_: (i, 0)),
                  pl.BlockSpec((n_s * N_LT, LANES), lambda i, *_: (0, 0))],
        out_specs=pl.BlockSpec(memory_space=pl.ANY),
        scratch_shapes=[pltpu.VMEM((MOE_BLK * N_LT, LANES), F32),
                        pltpu.SemaphoreType.DMA(()), pltpu.SemaphoreType.DMA(())])
    return pl.pallas_call(
        functools.partial(_dispatch_kernel, tm=tm, n_sample=n_s, n_blocks=n_blocks),
        out_shape=jax.ShapeDtypeStruct((n_blocks * MOE_BLK * N_LT, LANES), F32), grid_spec=grid_spec,
        compiler_params=_cparams(1, 32), name="moe_dispatch",
    )(d1_p, d2_p, d1_s, d2_s, fill, end, n_active, h2_p, h2_s)


def _experts_kernel(be_ref, na_ref, xb_ref, wg_ref, wu_ref, wd_ref, yb_ref, x_scr, wg_scr, wu_scr, wd_scr):
    b = pl.program_id(0)
    active = b < na_ref[0]
    new_expert = (b == 0) | (be_ref[b] != be_ref[jnp.maximum(b - 1, 0)])

    @pl.when(active & new_expert)
    def _():
        wg_scr[...] = wg_ref[...].astype(BF16)
        wu_scr[...] = wu_ref[...].astype(BF16)
        wd_scr[...] = wd_ref[...].astype(BF16)

    @pl.when(active)
    def _():
        for s in range(N_LT):
            x_scr[:, s * LANES:(s + 1) * LANES] = xb_ref[pl.ds(s, MOE_BLK, stride=N_LT), :]
        x = x_scr[...].astype(BF16)
        g = _dot(x, wg_scr[...])
        u = _dot(x, wu_scr[...])
        a = (g * (1.0 / (1.0 + jnp.exp(-g))) * u).astype(BF16)
        y = _dot(a, wd_scr[...])
        for s in range(N_LT):
            yb_ref[pl.ds(s, MOE_BLK, stride=N_LT), :] = y[:, s * LANES:(s + 1) * LANES]

    @pl.when(pl.program_id(0) >= na_ref[0])
    def _():
        yb_ref[...] = jnp.zeros(yb_ref.shape, F32)


def _experts(block_e, n_active, xb, wg, wu, wd, *, layer):
    n_slots = xb.shape[0] // N_LT
    n_blocks = n_slots // MOE_BLK
    blk = lambda b, be, na: (jnp.minimum(b, na[0] - 1), 0)
    wsel = lambda b, be, na: (layer, be[b], 0, 0)
    grid_spec = pltpu.PrefetchScalarGridSpec(
        num_scalar_prefetch=2, grid=(n_blocks,),
        in_specs=[
            pl.BlockSpec((MOE_BLK * N_LT, LANES), blk),
            pl.BlockSpec((None, None, D_MODEL, D_EXPERT), wsel),
            pl.BlockSpec((None, None, D_MODEL, D_EXPERT), wsel),
            pl.BlockSpec((None, None, D_EXPERT, D_MODEL), wsel),
        ],
        out_specs=pl.BlockSpec((MOE_BLK * N_LT, LANES), lambda b, be, na: (b, 0)),
        scratch_shapes=[pltpu.VMEM((MOE_BLK, D_MODEL), F32),
                        pltpu.VMEM((D_MODEL, D_EXPERT), BF16), pltpu.VMEM((D_MODEL, D_EXPERT), BF16),
                        pltpu.VMEM((D_EXPERT, D_MODEL), BF16)])
    return pl.pallas_call(
        _experts_kernel, out_shape=jax.ShapeDtypeStruct(xb.shape, F32), grid_spec=grid_spec,
        compiler_params=_cparams(1, 52), name="moe_experts",
    )(block_e, n_active, xb, wg, wu, wd)


def _combine_kernel(d1_ref, d2_ref, xm_ref, info_ref, yb_hbm, out_ref, a_scr, b_scr, sems, *, tm, n_tiles):
    i = pl.program_id(0)

    def fetch(tile, slot):
        base = tile * tm

        def issue(r):
            t = base + r
            _row_copy(_rows(yb_hbm, d1_ref[t]), _rows(a_scr.at[slot], r), sems.at[slot]).start()
            _row_copy(_rows(yb_hbm, d2_ref[t]), _rows(b_scr.at[slot], r), sems.at[slot]).start()

        _for_each_row(tm, issue)

    @pl.when(i == 0)
    def _():
        fetch(0, 0)

    @pl.when(i + 1 < n_tiles)
    def _():
        fetch(i + 1, (i + 1) % 2)

    slot = i % 2
    _row_copy(_rows(yb_hbm, 0, tm), a_scr.at[slot], sems.at[slot]).wait()
    _row_copy(_rows(yb_hbm, 0, tm), b_scr.at[slot], sems.at[slot]).wait()

    info = info_ref[...]
    w1 = info[:, INFO_W1:INFO_W1 + 1]
    w2 = info[:, INFO_W2:INFO_W2 + 1]
    for s in range(N_LT):
        cols = slice(s * LANES, (s + 1) * LANES)
        lane_tile = pl.ds(s, tm, stride=N_LT)
        out_ref[:, cols] = xm_ref[:, cols] + (a_scr[slot, lane_tile, :] * w1 + b_scr[slot, lane_tile, :] * w2)


def _combine(dest1, dest2, xm_buf, info_buf, yb, *, tm):
    n_rows = xm_buf.shape[0]
    n_tiles = n_rows // tm
    grid_spec = pltpu.PrefetchScalarGridSpec(
        num_scalar_prefetch=2, grid=(n_tiles,),
        in_specs=[
            pl.BlockSpec((tm, D_MODEL), lambda i, a, b: (i, 0)),
            pl.BlockSpec((tm, LANES), lambda i, a, b: (i, 0)),
            pl.BlockSpec(memory_space=pl.ANY),
        ],
        out_specs=pl.BlockSpec((tm, D_MODEL), lambda i, a, b: (i, 0)),
        scratch_shapes=[pltpu.VMEM((2, tm * N_LT, LANES), F32), pltpu.VMEM((2, tm * N_LT, LANES), F32),
                        pltpu.SemaphoreType.DMA((2,))])
    return pl.pallas_call(
        functools.partial(_combine_kernel, tm=tm, n_tiles=n_tiles),
        out_shape=jax.ShapeDtypeStruct((n_rows, D_MODEL), F32), grid_spec=grid_spec,
        compiler_params=_cparams(1, 32), name="moe_combine",
    )(dest1, dest2, xm_buf, info_buf, yb)


def _layer(l, xp, xs, cache_k, cache_v, state_conv, page_table, norm_mix, w_in, g_q, g_k,
           lam_q1, lam_k1, lam_q2, lam_k2, g_sub, conv_w, w_out, norm_ffn, w_rg, b_rg, w_re, b_re,
           w_gate, w_up, w_down):
    batch, seq, _ = xp.shape
    nb = xs.shape[0]
    n_pages = page_table.shape[1]
    past = n_pages * PAGE
    t_p = batch * seq
    n_tok = t_p + nb
    lam_init = 0.8 - 0.6 * math.exp(-0.3 * l)

    w_in_bf = w_in[l].astype(BF16)
    w_out_bf = w_out[l].astype(BF16)
    gmix = norm_mix[l].reshape(1, D_MODEL)
    gffn = norm_ffn[l].reshape(1, D_MODEL)
    gq = jnp.tile(g_q[l], ATT_W // QK_HALF).reshape(1, ATT_W)
    gk = jnp.tile(g_k[l], ATT_W // QK_HALF).reshape(1, ATT_W)
    gsub = g_sub[l].reshape(1, V_DIM)
    grp = jnp.arange(IN_CHUNK) // QK_HALF
    gsum = jnp.where(grp[:, None] == grp[None, :], 1.0 / QK_HALF, 0.0).astype(BF16)
    lq1, lk1, lq2, lk2 = (v[l].reshape(1, QK_HALF) for v in (lam_q1, lam_k1, lam_q2, lam_k2))
    slopes = 2.0 ** (-8.0 * jnp.arange(1, N_HEADS + 1, dtype=F32) / N_HEADS)
    lane = jnp.arange(LANES)
    sl = slopes * LOG2E
    s_hi = sl.astype(BF16).astype(F32)
    s_mid = (sl - s_hi).astype(BF16).astype(F32)
    s_lo = (sl - s_hi - s_mid).astype(BF16).astype(F32)
    terms = jnp.stack([s_hi, s_hi, s_mid, s_mid, s_lo, s_lo], axis=1)

    def query_alibi(first_lane):
        rel = lane - first_lane
        return jnp.where((rel >= 0) & (rel < ALIBI_LANES), terms[:, jnp.clip(rel, 0, ALIBI_LANES - 1)], 0.0)

    qal = jnp.stack([query_alibi(QK_HALF), query_alibi(0)], axis=1)
    w_r = jnp.zeros((D_MODEL, LANES), F32)
    w_r = w_r.at[:, 0:N_GROUPS].set(w_rg[l]).at[:, N_GROUPS:N_GROUPS + N_EXPERTS].set(w_re[l])
    wrh = w_r.astype(BF16)
    wr2 = jnp.concatenate([wrh, (w_r - wrh.astype(F32)).astype(BF16)], axis=1)
    br = jnp.zeros((1, LANES), F32)
    br = br.at[0, 0:N_GROUPS].set(b_rg[l]).at[0, N_GROUPS:N_GROUPS + N_EXPERTS].set(b_re[l])

    x2p = xp.reshape(t_p, D_MODEL)
    x2s = xs.reshape(nb, D_MODEL)
    q2, k2, vh, kf_p, vf_p, yc_p, cs_p = _in_proj_prompt(
        x2p, gmix, w_in_bf, gq, gk, gsum, conv_w[l], qal, batch=batch, seq=seq)
    o_p = _prompt_attention(q2, k2, vh, lq1, lk1, lq2, lk2, gsub, lam_init=lam_init)

    prev0 = state_conv[l, :, 0, :]
    prev1 = state_conv[l, :, 1, :]
    q_s, kf_s, vf_s, yc_s, u_s = _in_proj_sample(x2s, gmix, w_in_bf, gq, gk, gsum, conv_w[l], prev0, prev1)
    row_pos = jnp.arange(past * N_HEADS, dtype=jnp.int32) // N_HEADS
    row_head = jnp.arange(past * N_HEADS, dtype=jnp.int32) % N_HEADS
    q_head = jnp.arange(2 * N_HEADS, dtype=jnp.int32) % N_HEADS
    bias = -(slopes[q_head][:, None] * (past - row_pos).astype(F32)[None, :])
    bm = jnp.where(q_head[:, None] == row_head[None, :], bias, NEG_BIG)
    o_s = _sample_attention(page_table, cache_k, cache_v, q_s, kf_s, vf_s, bm, lq1, lk1, lq2, lk2, gsub,
                            layer=l, lam_init=lam_init)
    o_s = o_s.reshape(nb, ATT_W).astype(BF16)

    cnt0 = jnp.zeros((1, LANES), F32)
    tm_p, tm_s = 256, nb
    tri = lambda n: (jnp.arange(n)[:, None] > jnp.arange(n)[None, :]).astype(BF16)
    xm_p, h2_p, info_p, cnt1 = _out_proj(
        o_p.reshape(t_p, ATT_W), yc_p, x2p, w_out_bf, gffn, wr2, br, tri(tm_p), cnt0, tm=tm_p)
    xm_s, h2_s, info_s, cnt2 = _out_proj(
        o_s, yc_s, x2s, w_out_bf, gffn, wr2, br, tri(tm_s), cnt1, tm=tm_s)

    counts = cnt2[0, 0:N_EXPERTS].astype(jnp.int32)
    padded = (counts + MOE_BLK - 1) // MOE_BLK * MOE_BLK
    pad_end = jnp.cumsum(padded)
    pad_start = pad_end - padded
    n_blocks = -(-2 * n_tok // MOE_BLK) + N_EXPERTS
    blk_row = jnp.arange(n_blocks, dtype=jnp.int32) * MOE_BLK
    block_e = jnp.minimum(jnp.sum((pad_end[None, :] <= blk_row[:, None]).astype(jnp.int32), axis=1),
                          N_EXPERTS - 1)
    n_active = (pad_end[-1:] // MOE_BLK).astype(jnp.int32)
    experts = jnp.arange(N_EXPERTS, dtype=jnp.int32)

    def slots(info):
        def one(e_col, r_col):
            e = info[:, e_col].astype(jnp.int32)
            start = jnp.sum(jnp.where(e[:, None] == experts[None, :], pad_start[None, :], 0), axis=1)
            return start + info[:, r_col].astype(jnp.int32)
        return one(INFO_E1, INFO_R1), one(INFO_E2, INFO_R2)

    d1_p, d2_p = slots(info_p)
    d1_s, d2_s = slots(info_s)

    xb = _dispatch(d1_p, d2_p, d1_s, d2_s, pad_start + counts, pad_end, n_active, h2_p, h2_s,
                   n_blocks=n_blocks, tm=512)
    yb = _experts(block_e, n_active, xb, w_gate, w_up, w_down, layer=l)
    y_p = _combine(d1_p, d2_p, xm_p, info_p, yb, tm=128)
    y_s = _combine(d1_s, d2_s, xm_s, info_s, yb, tm=128)

    conv_s = jnp.stack([prev1, u_s], axis=1)
    return (y_p.reshape(batch, seq, D_MODEL), y_s.reshape(nb, 1, D_MODEL),
            kf_p.reshape(batch, seq, N_HEADS, V_DIM), vf_p.reshape(batch, seq, N_HEADS, V_DIM), cs_p,
            kf_s.reshape(nb, 1, N_HEADS, V_DIM), vf_s.reshape(nb, 1, N_HEADS, V_DIM), conv_s)


def kernel(x_prompt, x_sample, cache_k, cache_v, state_conv, page_table, norm_mix, w_in, g_q, g_k,
           lam_q1, lam_k1, lam_q2, lam_k2, g_sub, conv_w, w_out, norm_ffn, w_router_group, b_router_group,
           w_router_expert, b_router_expert, w_gate, w_up, w_down):
    depth = w_in.shape[0]
    xp, xs = x_prompt, x_sample
    outs = [[] for _ in range(6)]
    for l in range(depth):
        xp, xs, *rest = _layer(
            l, xp, xs, cache_k, cache_v, state_conv, page_table, norm_mix, w_in, g_q, g_k,
            lam_q1, lam_k1, lam_q2, lam_k2, g_sub, conv_w, w_out, norm_ffn, w_router_group, b_router_group,
            w_router_expert, b_router_expert, w_gate, w_up, w_down)
        for acc, val in zip(outs, rest):
            acc.append(val)
    return (xp, xs) + tuple(jnp.stack(v, axis=0) for v in outs)
```

```python
import functools
import math

import jax
import jax.numpy as jnp
from jax import lax
from jax.experimental import pallas as pl
from jax.experimental.pallas import tpu as pltpu

D_MODEL = 2048
N_HEADS = 8
V_DIM = 128
QK_HALF = 64
ATT_W = N_HEADS * V_DIM
CONV_W = D_MODEL - ATT_W
CONV_K = 3
IN_COLS = 3 * ATT_W + 3 * CONV_W
N_GROUPS = 4
EPG = 8
N_EXPERTS = N_GROUPS * EPG
D_EXPERT = D_MODEL // 4
PAGE = 128
EPS = 1e-6

LANES = 128
SUBLANES = 8
N_LT = D_MODEL // LANES
MOE_BLK = 128
NEG_BIG = -1e30
POS_SPLIT = 16
ALIBI_LANES = 6
LOG2E = math.log2(math.e)
PAGE_ROWS = PAGE * N_HEADS

F32 = jnp.float32
BF16 = jnp.bfloat16


def _cparams(n_axes, vmem_mb):
    return pltpu.CompilerParams(
        dimension_semantics=("arbitrary",) * n_axes,
        vmem_limit_bytes=vmem_mb * 1024 * 1024)


def _const_spec(shape, single_buffer=False):
    nd = len(shape)
    idx = lambda *_: (0,) * nd
    if single_buffer:
        return pl.BlockSpec(shape, idx, pipeline_mode=pl.Buffered(1))
    return pl.BlockSpec(shape, idx)


def _dot(a, b):
    return jnp.dot(a, b, preferred_element_type=F32)


def _dot_nt(a, b):
    return lax.dot_general(a, b, (((1,), (1,)), ((), ())), preferred_element_type=F32)


def _rms_rows(x, gain):
    ms = jnp.mean(x * x, axis=-1, keepdims=True)
    return x * lax.rsqrt(ms + EPS) * gain


def _diff_lambda(lq1, lk1, lq2, lk2, lam_init):
    s1 = jnp.sum(lq1 * lk1, axis=-1, keepdims=True)
    s2 = jnp.sum(lq2 * lk2, axis=-1, keepdims=True)
    return jnp.exp(s1) - jnp.exp(s2) + lam_init


IN_CHUNK = 512


def _half_norm(p, gsum, gain):
    ms = _dot((p * p).astype(BF16), gsum)
    return p * lax.rsqrt(ms + EPS) * gain


def _in_proj_qkv_kernel(x_ref, gmix_ref, w_ref, gq_ref, gk_ref, gsum_ref, qal_ref,
                        q2_ref, k2_ref, vh_ref, kf_ref, vf_ref, h_scr, *, tm, tiles_per_seq):
    i = pl.program_id(0)
    h_scr[...] = _rms_rows(x_ref[...], gmix_ref[...]).astype(BF16)
    gsum = gsum_ref[...]
    lane = lax.broadcasted_iota(jnp.int32, (tm, LANES), 1)
    first_half = lane < QK_HALF
    heads_per_chunk = IN_CHUNK // V_DIM

    pos = lax.broadcasted_iota(jnp.int32, (tm, LANES), 0) + (i % tiles_per_seq) * tm
    pos_hi = (pos // POS_SPLIT * POS_SPLIT).astype(F32)
    pos_lo = (pos % POS_SPLIT).astype(F32)

    def key_alibi(first_lane):
        rel = lane - first_lane
        return jnp.where((rel >= 0) & (rel < ALIBI_LANES), jnp.where(rel % 2 == 0, pos_hi, pos_lo), 0.0)

    kal_a = key_alibi(QK_HALF)
    kal_b = key_alibi(0)

    for c in range(ATT_W // IN_CHUNK):
        cols = slice(c * IN_CHUNK, (c + 1) * IN_CHUNK)
        p = _dot(h_scr[...], w_ref[:, c * IN_CHUNK:(c + 1) * IN_CHUNK])
        qn = _half_norm(p, gsum, gq_ref[:, cols]) * (LOG2E * QK_HALF ** -0.5)
        for hh in range(heads_per_chunk):
            qh = qn[:, hh * V_DIM:(hh + 1) * V_DIM]
            head = c * heads_per_chunk + hh
            q2_ref[head, 0] = jnp.where(first_half, qh, qal_ref[head, 0:1, :]).astype(BF16)
            q2_ref[head, 1] = jnp.where(first_half, qal_ref[head, 1:2, :], qh).astype(BF16)
        p = _dot(h_scr[...], w_ref[:, ATT_W + c * IN_CHUNK:ATT_W + (c + 1) * IN_CHUNK])
        kn = _half_norm(p, gsum, gk_ref[:, cols])
        kf_ref[:, cols] = kn
        for hh in range(heads_per_chunk):
            kh = kn[:, hh * V_DIM:(hh + 1) * V_DIM]
            head = c * heads_per_chunk + hh
            k2_ref[head, 0] = jnp.where(first_half, kh, kal_a).astype(BF16)
            k2_ref[head, 1] = jnp.where(first_half, kal_b, kh).astype(BF16)
        p = _dot(h_scr[...], w_ref[:, 2 * ATT_W + c * IN_CHUNK:2 * ATT_W + (c + 1) * IN_CHUNK])
        vf_ref[:, cols] = p
        for hh in range(heads_per_chunk):
            vh_ref[c * heads_per_chunk + hh] = p[:, hh * V_DIM:(hh + 1) * V_DIM].astype(BF16)


def _in_proj_conv_kernel(x_ref, gmix_ref, w_ref, cw_ref, yc_ref, cs_ref, h_scr, u_scr, *, tm, tiles_per_seq):
    i = pl.program_id(0)

    @pl.when(i % tiles_per_seq == 0)
    def _():
        u_scr[0:SUBLANES, :] = jnp.zeros((SUBLANES, CONV_W), F32)

    h_scr[...] = _rms_rows(x_ref[...], gmix_ref[...]).astype(BF16)
    for c in range(CONV_W // IN_CHUNK):
        cols = slice(c * IN_CHUNK, (c + 1) * IN_CHUNK)
        lo = c * IN_CHUNK
        bg = _dot(h_scr[...], w_ref[:, lo:lo + IN_CHUNK])
        cg = _dot(h_scr[...], w_ref[:, CONV_W + lo:CONV_W + lo + IN_CHUNK])
        xc = _dot(h_scr[...], w_ref[:, 2 * CONV_W + lo:2 * CONV_W + lo + IN_CHUNK])
        u = cg * xc
        u_scr[SUBLANES:SUBLANES + tm, cols] = u
        u1 = u_scr[SUBLANES - 1:SUBLANES - 1 + tm, cols]
        u2 = u_scr[SUBLANES - 2:SUBLANES - 2 + tm, cols]
        y = cw_ref[0:1, cols] * u2 + cw_ref[1:2, cols] * u1 + cw_ref[2:3, cols] * u
        yc_ref[:, cols] = (bg * y).astype(BF16)

    cs_ref[...] = u_scr[tm + SUBLANES - 2:tm + SUBLANES, :]
    u_scr[0:SUBLANES, :] = u_scr[tm:tm + SUBLANES, :]


def _in_proj_sample_kernel(x_ref, gmix_ref, w_ref, gq_ref, gk_ref, gsum_ref, cw_ref, p0_ref, p1_ref,
                           q_ref, kf_ref, vf_ref, yc_ref, u_ref, h_scr):
    h_scr[...] = _rms_rows(x_ref[...], gmix_ref[...]).astype(BF16)
    gsum = gsum_ref[...]
    heads_per_chunk = IN_CHUNK // V_DIM
    for c in range(ATT_W // IN_CHUNK):
        cols = slice(c * IN_CHUNK, (c + 1) * IN_CHUNK)
        p = _dot(h_scr[...], w_ref[:, c * IN_CHUNK:(c + 1) * IN_CHUNK])
        qn = _half_norm(p, gsum, gq_ref[:, cols]) * (QK_HALF ** -0.5)
        p = _dot(h_scr[...], w_ref[:, ATT_W + c * IN_CHUNK:ATT_W + (c + 1) * IN_CHUNK])
        kn = _half_norm(p, gsum, gk_ref[:, cols])
        vn = _dot(h_scr[...], w_ref[:, 2 * ATT_W + c * IN_CHUNK:2 * ATT_W + (c + 1) * IN_CHUNK])
        for hh in range(heads_per_chunk):
            head = c * heads_per_chunk + hh
            hcols = slice(hh * V_DIM, (hh + 1) * V_DIM)
            q_ref[:, head, :] = qn[:, hcols]
            kf_ref[:, head, :] = kn[:, hcols]
            vf_ref[:, head, :] = vn[:, hcols]
    base = 3 * ATT_W
    for c in range(CONV_W // IN_CHUNK):
        cols = slice(c * IN_CHUNK, (c + 1) * IN_CHUNK)
        lo = c * IN_CHUNK
        bg = _dot(h_scr[...], w_ref[:, base + lo:base + lo + IN_CHUNK])
        cg = _dot(h_scr[...], w_ref[:, base + CONV_W + lo:base + CONV_W + lo + IN_CHUNK])
        xc = _dot(h_scr[...], w_ref[:, base + 2 * CONV_W + lo:base + 2 * CONV_W + lo + IN_CHUNK])
        u = cg * xc
        u_ref[:, cols] = u
        y = cw_ref[0:1, cols] * p0_ref[:, cols] + cw_ref[1:2, cols] * p1_ref[:, cols] + cw_ref[2:3, cols] * u
        yc_ref[:, cols] = (bg * y).astype(BF16)


def _in_proj_prompt(x, gmix, w_bf, gq, gk, gsum, cw, qal, *, batch, seq, tm=512):
    t = batch * seq
    tps = seq // tm
    row = lambda i: (i, 0)
    half_cols = IN_COLS // 2
    x_spec = pl.BlockSpec((tm, D_MODEL), row)
    w_spec = lambda part: pl.BlockSpec((D_MODEL, half_cols), lambda i: (0, part), pipeline_mode=pl.Buffered(1))
    head_blk = lambda i: (i // tps, 0, 0, i % tps, 0)
    q2, k2, vh, kf, vf = pl.pallas_call(
        functools.partial(_in_proj_qkv_kernel, tm=tm, tiles_per_seq=tps),
        out_shape=(
            jax.ShapeDtypeStruct((batch, N_HEADS, 2, seq, V_DIM), BF16),
            jax.ShapeDtypeStruct((batch, N_HEADS, 2, seq, V_DIM), BF16),
            jax.ShapeDtypeStruct((batch, N_HEADS, seq, V_DIM), BF16),
            jax.ShapeDtypeStruct((t, ATT_W), F32),
            jax.ShapeDtypeStruct((t, ATT_W), F32),
        ),
        grid=(t // tm,),
        in_specs=[x_spec, _const_spec((1, D_MODEL)), w_spec(0), _const_spec((1, ATT_W)), _const_spec((1, ATT_W)),
                  _const_spec((IN_CHUNK, IN_CHUNK)), _const_spec((N_HEADS, 2, LANES))],
        out_specs=(
            pl.BlockSpec((None, N_HEADS, 2, tm, V_DIM), head_blk),
            pl.BlockSpec((None, N_HEADS, 2, tm, V_DIM), head_blk),
            pl.BlockSpec((None, N_HEADS, tm, V_DIM), lambda i: (i // tps, 0, i % tps, 0)),
            pl.BlockSpec((tm, ATT_W), row),
            pl.BlockSpec((tm, ATT_W), row),
        ),
        scratch_shapes=[pltpu.VMEM((tm, D_MODEL), BF16)],
        compiler_params=_cparams(1, 56), name="in_proj_qkv",
    )(x, gmix, w_bf, gq, gk, gsum, qal)
    yc, cs = pl.pallas_call(
        functools.partial(_in_proj_conv_kernel, tm=tm, tiles_per_seq=tps),
        out_shape=(
            jax.ShapeDtypeStruct((t, CONV_W), BF16),
            jax.ShapeDtypeStruct((batch, CONV_K - 1, CONV_W), F32),
        ),
        grid=(t // tm,),
        in_specs=[x_spec, _const_spec((1, D_MODEL)), w_spec(1), _const_spec((CONV_K, CONV_W))],
        out_specs=(
            pl.BlockSpec((tm, CONV_W), row),
            pl.BlockSpec((None, CONV_K - 1, CONV_W), lambda i: (i // tps, 0, 0)),
        ),
        scratch_shapes=[pltpu.VMEM((tm, D_MODEL), BF16), pltpu.VMEM((tm + SUBLANES, CONV_W), F32)],
        compiler_params=_cparams(1, 48), name="in_proj_conv",
    )(x, gmix, w_bf, cw)
    return q2, k2, vh, kf, vf, yc, cs


def _in_proj_sample(x, gmix, w_bf, gq, gk, gsum, cw, prev0, prev1):
    t = x.shape[0]
    full = lambda shape: _const_spec(shape)
    out_shape = (
        jax.ShapeDtypeStruct((t, N_HEADS, V_DIM), F32),
        jax.ShapeDtypeStruct((t, N_HEADS, V_DIM), F32),
        jax.ShapeDtypeStruct((t, N_HEADS, V_DIM), F32),
        jax.ShapeDtypeStruct((t, CONV_W), BF16),
        jax.ShapeDtypeStruct((t, CONV_W), F32),
    )
    in_specs = [
        full((t, D_MODEL)), full((1, D_MODEL)), _const_spec((D_MODEL, IN_COLS), single_buffer=True),
        full((1, ATT_W)), full((1, ATT_W)), full((IN_CHUNK, IN_CHUNK)), full((CONV_K, CONV_W)),
        full((t, CONV_W)), full((t, CONV_W)),
    ]
    return pl.pallas_call(
        _in_proj_sample_kernel, out_shape=out_shape, grid=(1,), in_specs=in_specs,
        out_specs=tuple(full(s.shape) for s in out_shape),
        scratch_shapes=[pltpu.VMEM((t, D_MODEL), BF16)],
        compiler_params=_cparams(1, 48), name="in_proj_sample",
    )(x, gmix, w_bf, gq, gk, gsum, cw, prev0, prev1)


def _prompt_query_block(i, lam, q2_ref, k2_ref, v_ref, gsub_ref, o_ref, s_scr, v2_scr, acc_scr, *, tq, lam_init):
    n_lt = tq // LANES
    row = lax.broadcasted_iota(jnp.int32, (tq, tq), 0)
    col = lax.broadcasted_iota(jnp.int32, (tq, tq), 1)
    causal = col <= row

    def lane_tile_max(m_acc, s):
        for t in range(n_lt):
            m_acc = jnp.maximum(m_acc, s[:, t * LANES:(t + 1) * LANES])
        return m_acc

    def lane_tile_sum(l_acc, p):
        for t in range(n_lt):
            l_acc = l_acc + p[:, t * LANES:(t + 1) * LANES]
        return l_acc

    def scores(j, c):
        start = pl.multiple_of(j * tq, tq)
        return _dot_nt(q2_ref[c], k2_ref[c, pl.ds(start, tq), :])

    def pass1(j, carry):
        out = []
        for c in range(2):
            s = scores(j, c)
            s_scr[c, j] = s
            out.append(lane_tile_max(carry[c], s))
        return tuple(out)

    neg = jnp.full((tq, LANES), NEG_BIG, F32)
    m_acc = lax.fori_loop(0, i, pass1, (neg, neg))
    m_rows = []
    for c in range(2):
        s = jnp.where(causal, scores(i, c), NEG_BIG)
        s_scr[c, i] = s
        m_rows.append(jnp.max(lane_tile_max(m_acc[c], s), axis=-1, keepdims=True))

    @pl.when(i == 0)
    def _():
        v2_scr[...] = jnp.zeros(v2_scr.shape, BF16)
        for j in range(v2_scr.shape[0]):
            v = v_ref[j * tq:(j + 1) * tq, :]
            v2_scr[j, 0:tq, 0:V_DIM] = v
            v2_scr[j, tq:2 * tq, V_DIM:2 * V_DIM] = v

    acc_scr[...] = jnp.zeros(acc_scr.shape, F32)

    def pass2(j, carry):
        l0, l1 = carry
        p0 = jnp.exp2(s_scr[0, j] - m_rows[0])
        p1 = jnp.exp2(s_scr[1, j] - m_rows[1])
        pcat = jnp.concatenate([p0.astype(BF16), p1.astype(BF16)], axis=1)
        acc_scr[...] += _dot(pcat, v2_scr[j])
        return lane_tile_sum(l0, p0), lane_tile_sum(l1, p1)

    z = jnp.zeros((tq, LANES), F32)
    l0, l1 = lax.fori_loop(0, i + 1, pass2, (z, z))
    a0 = acc_scr[:, 0:V_DIM]
    a1 = acc_scr[:, V_DIM:2 * V_DIM]
    o = a0 / jnp.sum(l0, axis=-1, keepdims=True) - lam * (a1 / jnp.sum(l1, axis=-1, keepdims=True))
    o_ref[...] = (_rms_rows(o, gsub_ref[...]) * (1.0 - lam_init)).astype(BF16)


SAMPLE_PARTS = 2


def _sample_pages(part, lam, k_refs, v_refs, q_ref, kn_ref, vn_ref, bm_ref, gsub_ref, o_ref,
                  m_scr, l_scr, acc_scr, *, lam_init):
    n_rows = 2 * N_HEADS
    n_pages = len(k_refs)

    @pl.when(part == 0)
    def _():
        m_scr[...] = jnp.full(m_scr.shape, NEG_BIG, F32)
        l_scr[...] = jnp.zeros(l_scr.shape, F32)
        acc_scr[...] = jnp.zeros(acc_scr.shape, F32)

    q8 = q_ref[...]
    lane = lax.broadcasted_iota(jnp.int32, (N_HEADS, V_DIM), 1)
    qm = jnp.concatenate([jnp.where(lane < QK_HALF, q8, 0.0), jnp.where(lane < QK_HALF, 0.0, q8)], axis=0)
    qm_bf = qm.astype(BF16)

    s_parts = [_dot_nt(qm_bf, k_refs[p][...].reshape(PAGE_ROWS, V_DIM).astype(BF16)) for p in range(n_pages)]
    s = jnp.concatenate(s_parts, axis=-1) + bm_ref[...]
    kn = kn_ref[...].astype(BF16).astype(F32)
    s_new = jnp.sum(qm_bf.astype(F32) * jnp.concatenate([kn, kn], axis=0), axis=-1, keepdims=True)
    s_new = jnp.where(part == SAMPLE_PARTS - 1, s_new, NEG_BIG)

    m_prev = m_scr[...]
    m = jnp.maximum(jnp.maximum(jnp.max(s, axis=-1, keepdims=True), s_new), m_prev)
    p = jnp.exp(s - m)
    p_new = jnp.exp(s_new - m)
    keep = jnp.exp(m_prev - m)
    l = keep * l_scr[...] + jnp.sum(p, axis=-1, keepdims=True) + p_new
    p_bf = p.astype(BF16)
    acc = keep * acc_scr[...]
    for pg in range(n_pages):
        acc = acc + _dot(p_bf[:, pg * PAGE_ROWS:(pg + 1) * PAGE_ROWS],
                         v_refs[pg][...].reshape(PAGE_ROWS, V_DIM).astype(BF16))
    vn = vn_ref[...].astype(BF16).astype(F32)
    acc = acc + p_new.astype(BF16).astype(F32) * jnp.concatenate([vn, vn], axis=0)
    m_scr[...] = m
    l_scr[...] = l
    acc_scr[...] = acc

    w = acc / l
    o = w[0:N_HEADS] - lam * w[N_HEADS:n_rows]
    o_ref[...] = _rms_rows(o, gsub_ref[...]) * (1.0 - lam_init)


def _attention_kernel(pt_ref, q2_ref, k2_ref, v_ref, *refs, tq, n_page_refs, lam_init):
    del pt_ref
    k_refs = refs[:n_page_refs]
    v_refs = refs[n_page_refs:2 * n_page_refs]
    (qs_ref, kn_ref, vn_ref, bm_ref, lq1_ref, lk1_ref, lq2_ref, lk2_ref, gsub_ref,
     o_ref, os_ref, s_scr, v2_scr, acc_scr, sm_scr, sl_scr, sacc_scr) = refs[2 * n_page_refs:]
    i = pl.program_id(2)
    lam = _diff_lambda(lq1_ref[...], lk1_ref[...], lq2_ref[...], lk2_ref[...], lam_init)
    _prompt_query_block(i, lam, q2_ref, k2_ref, v_ref, gsub_ref, o_ref, s_scr, v2_scr, acc_scr,
                        tq=tq, lam_init=lam_init)
    _sample_pages(i % SAMPLE_PARTS, lam, k_refs, v_refs, qs_ref, kn_ref, vn_ref, bm_ref, gsub_ref, os_ref,
                  sm_scr, sl_scr, sacc_scr, lam_init=lam_init)


def _attention(page_table, q2, k2, vh, ck, cv, q_s, kn_s, vn_s, bm, lq1, lk1, lq2, lk2, gsub, *,
               layer, lam_init, tq=512):
    batch, _, _, seq, _ = q2.shape
    nq = seq // tq
    nb, n_pages = page_table.shape
    assert nq % SAMPLE_PARTS == 0 and batch * N_HEADS * nq == nb * SAMPLE_PARTS
    assert n_pages % SAMPLE_PARTS == 0
    pps = n_pages // SAMPLE_PARTS
    seqs_per_head = nq // SAMPLE_PARTS
    pt_flat = page_table.reshape(nb * n_pages)

    def sample_seq(b, h, i):
        return (b * N_HEADS + h) * seqs_per_head + i // SAMPLE_PARTS

    def page_spec(p):
        def idx(b, h, i, pt):
            return (layer, pt[sample_seq(b, h, i) * n_pages + (i % SAMPLE_PARTS) * pps + p], 0, 0, 0)
        return pl.BlockSpec((None, None, PAGE, N_HEADS, V_DIM), idx)

    srow = pl.BlockSpec((None, N_HEADS, V_DIM), lambda b, h, i, pt: (sample_seq(b, h, i), 0, 0))
    vec = lambda n: pl.BlockSpec((1, n), lambda b, h, i, pt: (0, 0))
    in_specs = (
        [pl.BlockSpec((None, None, 2, tq, V_DIM), lambda b, h, i, pt: (b, h, 0, i, 0)),
         pl.BlockSpec((None, None, 2, seq, V_DIM), lambda b, h, i, pt: (b, h, 0, 0, 0)),
         pl.BlockSpec((None, None, seq, V_DIM), lambda b, h, i, pt: (b, h, 0, 0))]
        + [page_spec(p) for p in range(pps)] + [page_spec(p) for p in range(pps)]
        + [srow, srow, srow,
           pl.BlockSpec((None, 2 * N_HEADS, pps * PAGE_ROWS), lambda b, h, i, pt: (i % SAMPLE_PARTS, 0, 0)),
           vec(QK_HALF), vec(QK_HALF), vec(QK_HALF), vec(QK_HALF), vec(V_DIM)])
    grid_spec = pltpu.PrefetchScalarGridSpec(
        num_scalar_prefetch=1, grid=(batch, N_HEADS, nq), in_specs=in_specs,
        out_specs=(pl.BlockSpec((None, tq, V_DIM), lambda b, h, i, pt: (b, i, h)), srow),
        scratch_shapes=[pltpu.VMEM((2, nq, tq, tq), F32), pltpu.VMEM((nq, 2 * tq, 2 * V_DIM), BF16),
                        pltpu.VMEM((tq, 2 * V_DIM), F32),
                        pltpu.VMEM((2 * N_HEADS, 1), F32), pltpu.VMEM((2 * N_HEADS, 1), F32),
                        pltpu.VMEM((2 * N_HEADS, V_DIM), F32)])
    return pl.pallas_call(
        functools.partial(_attention_kernel, tq=tq, n_page_refs=pps, lam_init=lam_init),
        out_shape=(jax.ShapeDtypeStruct((batch, seq, ATT_W), BF16),
                   jax.ShapeDtypeStruct((nb, N_HEADS, V_DIM), F32)),
        grid_spec=grid_spec, compiler_params=_cparams(3, 52), name="attention",
    )(pt_flat, q2, k2, vh, *([ck] * pps), *([cv] * pps), q_s, kn_s, vn_s, bm, lq1, lk1, lq2, lk2, gsub)


INFO_E1, INFO_E2, INFO_W1, INFO_W2, INFO_R1, INFO_R2 = 0, 1, 2, 3, 4, 5


def _lane_min_index(mask, lane):
    return jnp.min(jnp.where(mask, lane, float(LANES)), axis=-1, keepdims=True)


def _out_proj_kernel(o_ref, yc_ref, x_ref, w_ref, gffn_ref, wr2_ref, br_ref, tri_ref, cnt_in_ref,
                     xm_ref, h2_ref, info_ref, cnt_ref, cnt_scr, *, tm):
    i = pl.program_id(0)

    @pl.when(i == 0)
    def _():
        cnt_scr[...] = cnt_in_ref[...]

    xm = x_ref[...] + _dot(o_ref[...], w_ref[0:ATT_W, :]) + _dot(yc_ref[...], w_ref[ATT_W:D_MODEL, :])
    xm_ref[...] = xm
    h2 = _rms_rows(xm, gffn_ref[...])
    for s in range(N_LT):
        h2_ref[pl.ds(s, tm, stride=N_LT), :] = h2[:, s * LANES:(s + 1) * LANES]

    hh = h2.astype(BF16)
    hl = (h2 - hh.astype(F32)).astype(BF16)
    parts = _dot(jnp.concatenate([hh, hl], axis=0), wr2_ref[...])
    logits = (parts[0:tm, 0:LANES] + parts[0:tm, LANES:2 * LANES]
              + parts[tm:2 * tm, 0:LANES] + parts[tm:2 * tm, LANES:2 * LANES]) + br_ref[...]

    lane = lax.broadcasted_iota(jnp.int32, (tm, LANES), 1).astype(F32)
    is_grp = lane < N_GROUPS
    lg = jnp.where(is_grp, logits, NEG_BIG)
    mg = jnp.max(lg, axis=-1, keepdims=True)
    g_star = _lane_min_index(is_grp & (lg == mg), lane)
    p_group = 1.0 / jnp.sum(jnp.where(is_grp, jnp.exp(lg - mg), 0.0), axis=-1, keepdims=True)

    e_lo = N_GROUPS + g_star * EPG
    in_grp = (lane >= e_lo) & (lane < e_lo + EPG)
    le = jnp.where(in_grp, logits, NEG_BIG)
    m1 = jnp.max(le, axis=-1, keepdims=True)
    ex = jnp.where(in_grp, jnp.exp(le - m1), 0.0)
    pe = ex / jnp.sum(ex, axis=-1, keepdims=True)
    pe_m = jnp.where(in_grp, pe, -1.0)
    p1 = jnp.max(pe_m, axis=-1, keepdims=True)
    i1 = _lane_min_index(pe_m == p1, lane)
    pe_m2 = jnp.where(lane == i1, -1.0, pe_m)
    p2 = jnp.max(pe_m2, axis=-1, keepdims=True)
    i2 = _lane_min_index(pe_m2 == p2, lane)
    denom = p1 + p2
    w1 = p_group * p1 / denom
    w2 = p_group * p2 / denom
    e1 = i1 - N_GROUPS
    e2 = i2 - N_GROUPS

    oh1 = lane == e1
    oh2 = lane == e2
    oh = jnp.where(oh1 | oh2, 1.0, 0.0)
    prefix = _dot(tri_ref[...], oh.astype(BF16)) + cnt_scr[...]
    r1 = jnp.sum(jnp.where(oh1, prefix, 0.0), axis=-1, keepdims=True)
    r2 = jnp.sum(jnp.where(oh2, prefix, 0.0), axis=-1, keepdims=True)
    cnt_scr[...] = cnt_scr[...] + jnp.sum(oh, axis=0, keepdims=True)
    cnt_ref[...] = cnt_scr[...]

    info = jnp.zeros((tm, LANES), F32)
    for idx, val in ((INFO_E1, e1), (INFO_E2, e2), (INFO_W1, w1), (INFO_W2, w2),
                     (INFO_R1, r1), (INFO_R2, r2)):
        info = jnp.where(lane == idx, val, info)
    info_ref[...] = info


def _out_proj(o, yc, x, w_bf, gffn, wr2, br, tri, cnt_in, *, tm):
    t = x.shape[0]
    row = lambda i: (i, 0)
    in_specs = [
        pl.BlockSpec((tm, ATT_W), row), pl.BlockSpec((tm, CONV_W), row), pl.BlockSpec((tm, D_MODEL), row),
        _const_spec((D_MODEL, D_MODEL), single_buffer=True), _const_spec((1, D_MODEL)),
        _const_spec((D_MODEL, 2 * LANES)), _const_spec((1, LANES)),
        _const_spec((tm, tm)), _const_spec((1, LANES)),
    ]
    out_specs = (
        pl.BlockSpec((tm, D_MODEL), row),
        pl.BlockSpec((tm * N_LT, LANES), row),
        pl.BlockSpec((tm, LANES), row),
        _const_spec((1, LANES)),
    )
    out_shape = (
        jax.ShapeDtypeStruct((t, D_MODEL), F32), jax.ShapeDtypeStruct((t * N_LT, LANES), F32),
        jax.ShapeDtypeStruct((t, LANES), F32), jax.ShapeDtypeStruct((1, LANES), F32),
    )
    return pl.pallas_call(
        functools.partial(_out_proj_kernel, tm=tm),
        out_shape=out_shape, grid=(t // tm,), in_specs=in_specs, out_specs=out_specs,
        scratch_shapes=[pltpu.VMEM((1, LANES), F32)],
        compiler_params=_cparams(1, 56), name="out_proj",
    )(o, yc, x, w_bf, gffn, wr2, br, tri, cnt_in)


def _row_copy(src, dst, sem):
    return pltpu.make_async_copy(src, dst, sem)


ISSUE_UNROLL = 8


def _for_each_row(n, fn):
    def chunk(k, c):
        for u in range(ISSUE_UNROLL):
            fn(k * ISSUE_UNROLL + u)
        return c
    lax.fori_loop(0, n // ISSUE_UNROLL, chunk, 0)


def _rows(ref, first, n=1):
    return ref.at[pl.ds(pl.multiple_of(first * N_LT, N_LT), n * N_LT)]


def _dispatch_kernel(d1_ref, d2_ref, s1_ref, s2_ref, fill_ref, end_ref, na_ref,
                     h2p_ref, h2s_ref, xb_hbm, zero_scr, sem, zsem, *, tm, n_sample, n_blocks):
    step = pl.program_id(0)

    @pl.when(step == 0)
    def _():
        zero_scr[...] = jnp.zeros(zero_scr.shape, F32)

        def pad_rows(fn):
            def per_expert(e, c):
                return lax.fori_loop(fill_ref[e], end_ref[e], lambda r, c2: fn(r, c2), c)
            lax.fori_loop(0, N_EXPERTS, per_expert, 0)

        def tail_blocks(fn):
            lax.fori_loop(na_ref[0], n_blocks, lambda b, c: fn(b, c), 0)

        def start_row(r, c):
            _row_copy(_rows(zero_scr, 0), _rows(xb_hbm, r), zsem).start()
            return c

        def wait_row(r, c):
            _row_copy(_rows(zero_scr, 0), _rows(xb_hbm, 0), zsem).wait()
            return c

        def start_blk(b, c):
            _row_copy(zero_scr, _rows(xb_hbm, b * MOE_BLK, MOE_BLK), zsem).start()
            return c

        def wait_blk(b, c):
            _row_copy(zero_scr, _rows(xb_hbm, 0, MOE_BLK), zsem).wait()
            return c

        pad_rows(start_row)
        tail_blocks(start_blk)

        def sample_row(r, c):
            _row_copy(_rows(h2s_ref, r), _rows(xb_hbm, s1_ref[r]), zsem).start()
            _row_copy(_rows(h2s_ref, r), _rows(xb_hbm, s2_ref[r]), zsem).start()
            return c

        lax.fori_loop(0, n_sample, sample_row, 0)
        pad_rows(wait_row)
        tail_blocks(wait_blk)
        _row_copy(h2s_ref, _rows(xb_hbm, 0, n_sample), zsem).wait()
        _row_copy(h2s_ref, _rows(xb_hbm, 0, n_sample), zsem).wait()

    base = step * tm

    def issue(r):
        t = base + r
        _row_copy(_rows(h2p_ref, r), _rows(xb_hbm, d1_ref[t]), sem).start()
        _row_copy(_rows(h2p_ref, r), _rows(xb_hbm, d2_ref[t]), sem).start()

    _for_each_row(tm, issue)
    _row_copy(h2p_ref, _rows(xb_hbm, 0, tm), sem).wait()
    _row_copy(h2p_ref, _rows(xb_hbm, 0, tm), sem).wait()


def _dispatch(d1_p, d2_p, d1_s, d2_s, fill, end, n_active, h2_p, h2_s, *, n_blocks, tm):
    n_p = h2_p.shape[0] // N_LT
    n_s = h2_s.shape[0] // N_LT
    grid_spec = pltpu.PrefetchScalarGridSpec(
        num_scalar_prefetch=7, grid=(n_p // tm,),
        in_specs=[pl.BlockSpec((tm * N_LT, LANES), lambda i, *_: (i, 0)),
                  pl.BlockSpec((n_s * N_LT, LANES), lambda i, *_: (0, 0))],
        out_specs=pl.BlockSpec(memory_space=pl.ANY),
        scratch_shapes=[pltpu.VMEM((MOE_BLK * N_LT, LANES), F32),
                        pltpu.SemaphoreType.DMA(()), pltpu.SemaphoreType.DMA(())])
    return pl.pallas_call(
        functools.partial(_dispatch_kernel, tm=tm, n_sample=n_s, n_blocks=n_blocks),
        out_shape=jax.ShapeDtypeStruct((n_blocks * MOE_BLK * N_LT, LANES), F32), grid_spec=grid_spec,
        compiler_params=_cparams(1, 32), name="moe_dispatch",
    )(d1_p, d2_p, d1_s, d2_s, fill, end, n_active, h2_p, h2_s)


def _experts_kernel(be_ref, na_ref, nxt_ref, xb_ref, wg_hbm, wu_hbm, wd_hbm, yb_ref,
                    x_scr, wg_f32, wu_f32, wd_f32, wg_scr, wu_scr, wd_scr, slot_ref, sems, *, layer):
    b = pl.program_id(0)
    active = b < na_ref[0]
    expert = be_ref[b]
    new_expert = (b == 0) | (expert != be_ref[jnp.maximum(b - 1, 0)])

    def weight_copies(e, slot):
        return [pltpu.make_async_copy(src.at[layer, e], dst.at[slot], sems.at[slot])
                for src, dst in ((wg_hbm, wg_f32), (wu_hbm, wu_f32), (wd_hbm, wd_f32))]

    @pl.when(b == 0)
    def _():
        slot_ref[0] = 0
        for cp in weight_copies(expert, 0):
            cp.start()

    @pl.when(active & new_expert)
    def _():
        slot = slot_ref[0]
        for cp in weight_copies(expert, slot):
            cp.wait()
        wg_scr[...] = wg_f32[slot].astype(BF16)
        wu_scr[...] = wu_f32[slot].astype(BF16)
        wd_scr[...] = wd_f32[slot].astype(BF16)
        nxt = nxt_ref[expert]

        @pl.when(nxt >= 0)
        def _():
            for cp in weight_copies(nxt, 1 - slot):
                cp.start()

        slot_ref[0] = 1 - slot

    @pl.when(active)
    def _():
        for s in range(N_LT):
            x_scr[:, s * LANES:(s + 1) * LANES] = xb_ref[pl.ds(s, MOE_BLK, stride=N_LT), :]
        x = x_scr[...].astype(BF16)
        g = _dot(x, wg_scr[...])
        u = _dot(x, wu_scr[...])
        a = (g * (1.0 / (1.0 + jnp.exp(-g))) * u).astype(BF16)
        y = _dot(a, wd_scr[...])
        for s in range(N_LT):
            yb_ref[pl.ds(s, MOE_BLK, stride=N_LT), :] = y[:, s * LANES:(s + 1) * LANES]

    @pl.when(pl.program_id(0) >= na_ref[0])
    def _():
        yb_ref[...] = jnp.zeros(yb_ref.shape, F32)


def _experts(block_e, n_active, next_expert, xb, wg, wu, wd, *, layer):
    n_slots = xb.shape[0] // N_LT
    n_blocks = n_slots // MOE_BLK
    blk = lambda b, be, na, nx: (jnp.minimum(b, na[0] - 1), 0)
    grid_spec = pltpu.PrefetchScalarGridSpec(
        num_scalar_prefetch=3, grid=(n_blocks,),
        in_specs=[
            pl.BlockSpec((MOE_BLK * N_LT, LANES), blk),
            pl.BlockSpec(memory_space=pl.ANY), pl.BlockSpec(memory_space=pl.ANY), pl.BlockSpec(memory_space=pl.ANY),
        ],
        out_specs=pl.BlockSpec((MOE_BLK * N_LT, LANES), lambda b, be, na, nx: (b, 0)),
        scratch_shapes=[pltpu.VMEM((MOE_BLK, D_MODEL), F32),
                        pltpu.VMEM((2, D_MODEL, D_EXPERT), F32), pltpu.VMEM((2, D_MODEL, D_EXPERT), F32),
                        pltpu.VMEM((2, D_EXPERT, D_MODEL), F32),
                        pltpu.VMEM((D_MODEL, D_EXPERT), BF16), pltpu.VMEM((D_MODEL, D_EXPERT), BF16),
                        pltpu.VMEM((D_EXPERT, D_MODEL), BF16),
                        pltpu.SMEM((1,), jnp.int32), pltpu.SemaphoreType.DMA((2,))])
    return pl.pallas_call(
        functools.partial(_experts_kernel, layer=layer),
        out_shape=jax.ShapeDtypeStruct(xb.shape, F32), grid_spec=grid_spec,
        compiler_params=_cparams(1, 52), name="moe_experts",
    )(block_e, n_active, next_expert, xb, wg, wu, wd)


def _combine_kernel(d1_ref, d2_ref, xm_ref, info_ref, yb_hbm, out_ref, a_scr, b_scr, sems, *, tm, n_tiles):
    i = pl.program_id(0)

    def fetch(tile, slot):
        base = tile * tm

        def issue(r):
            t = base + r
            _row_copy(_rows(yb_hbm, d1_ref[t]), _rows(a_scr.at[slot], r), sems.at[slot]).start()
            _row_copy(_rows(yb_hbm, d2_ref[t]), _rows(b_scr.at[slot], r), sems.at[slot]).start()

        _for_each_row(tm, issue)

    @pl.when(i == 0)
    def _():
        fetch(0, 0)

    @pl.when(i + 1 < n_tiles)
    def _():
        fetch(i + 1, (i + 1) % 2)

    slot = i % 2
    _row_copy(_rows(yb_hbm, 0, tm), a_scr.at[slot], sems.at[slot]).wait()
    _row_copy(_rows(yb_hbm, 0, tm), b_scr.at[slot], sems.at[slot]).wait()

    info = info_ref[...]
    w1 = info[:, INFO_W1:INFO_W1 + 1]
    w2 = info[:, INFO_W2:INFO_W2 + 1]
    for s in range(N_LT):
        cols = slice(s * LANES, (s + 1) * LANES)
        lane_tile = pl.ds(s, tm, stride=N_LT)
        out_ref[:, cols] = xm_ref[:, cols] + (a_scr[slot, lane_tile, :] * w1 + b_scr[slot, lane_tile, :] * w2)


def _combine(dest1, dest2, xm_buf, info_buf, yb, *, tm):
    n_rows = xm_buf.shape[0]
    n_tiles = n_rows // tm
    grid_spec = pltpu.PrefetchScalarGridSpec(
        num_scalar_prefetch=2, grid=(n_tiles,),
        in_specs=[
            pl.BlockSpec((tm, D_MODEL), lambda i, a, b: (i, 0)),
            pl.BlockSpec((tm, LANES), lambda i, a, b: (i, 0)),
            pl.BlockSpec(memory_space=pl.ANY),
        ],
        out_specs=pl.BlockSpec((tm, D_MODEL), lambda i, a, b: (i, 0)),
        scratch_shapes=[pltpu.VMEM((2, tm * N_LT, LANES), F32), pltpu.VMEM((2, tm * N_LT, LANES), F32),
                        pltpu.SemaphoreType.DMA((2,))])
    return pl.pallas_call(
        functools.partial(_combine_kernel, tm=tm, n_tiles=n_tiles),
        out_shape=jax.ShapeDtypeStruct((n_rows, D_MODEL), F32), grid_spec=grid_spec,
        compiler_params=_cparams(1, 32), name="moe_combine",
    )(dest1, dest2, xm_buf, info_buf, yb)


def _layer(l, xp, xs, cache_k, cache_v, state_conv, page_table, norm_mix, w_in, g_q, g_k,
           lam_q1, lam_k1, lam_q2, lam_k2, g_sub, conv_w, w_out, norm_ffn, w_rg, b_rg, w_re, b_re,
           w_gate, w_up, w_down):
    batch, seq, _ = xp.shape
    nb = xs.shape[0]
    n_pages = page_table.shape[1]
    past = n_pages * PAGE
    t_p = batch * seq
    n_tok = t_p + nb
    lam_init = 0.8 - 0.6 * math.exp(-0.3 * l)

    w_in_bf = w_in[l].astype(BF16)
    w_out_bf = w_out[l].astype(BF16)
    gmix = norm_mix[l].reshape(1, D_MODEL)
    gffn = norm_ffn[l].reshape(1, D_MODEL)
    gq = jnp.tile(g_q[l], ATT_W // QK_HALF).reshape(1, ATT_W)
    gk = jnp.tile(g_k[l], ATT_W // QK_HALF).reshape(1, ATT_W)
    gsub = g_sub[l].reshape(1, V_DIM)
    grp = jnp.arange(IN_CHUNK) // QK_HALF
    gsum = jnp.where(grp[:, None] == grp[None, :], 1.0 / QK_HALF, 0.0).astype(BF16)
    lq1, lk1, lq2, lk2 = (v[l].reshape(1, QK_HALF) for v in (lam_q1, lam_k1, lam_q2, lam_k2))
    slopes = 2.0 ** (-8.0 * jnp.arange(1, N_HEADS + 1, dtype=F32) / N_HEADS)
    lane = jnp.arange(LANES)
    sl = slopes * LOG2E
    s_hi = sl.astype(BF16).astype(F32)
    s_mid = (sl - s_hi).astype(BF16).astype(F32)
    s_lo = (sl - s_hi - s_mid).astype(BF16).astype(F32)
    terms = jnp.stack([s_hi, s_hi, s_mid, s_mid, s_lo, s_lo], axis=1)

    def query_alibi(first_lane):
        rel = lane - first_lane
        return jnp.where((rel >= 0) & (rel < ALIBI_LANES), terms[:, jnp.clip(rel, 0, ALIBI_LANES - 1)], 0.0)

    qal = jnp.stack([query_alibi(QK_HALF), query_alibi(0)], axis=1)
    w_r = jnp.zeros((D_MODEL, LANES), F32)
    w_r = w_r.at[:, 0:N_GROUPS].set(w_rg[l]).at[:, N_GROUPS:N_GROUPS + N_EXPERTS].set(w_re[l])
    wrh = w_r.astype(BF16)
    wr2 = jnp.concatenate([wrh, (w_r - wrh.astype(F32)).astype(BF16)], axis=1)
    br = jnp.zeros((1, LANES), F32)
    br = br.at[0, 0:N_GROUPS].set(b_rg[l]).at[0, N_GROUPS:N_GROUPS + N_EXPERTS].set(b_re[l])

    x2p = xp.reshape(t_p, D_MODEL)
    x2s = xs.reshape(nb, D_MODEL)
    q2, k2, vh, kf_p, vf_p, yc_p, cs_p = _in_proj_prompt(
        x2p, gmix, w_in_bf, gq, gk, gsum, conv_w[l], qal, batch=batch, seq=seq)

    prev0 = state_conv[l, :, 0, :]
    prev1 = state_conv[l, :, 1, :]
    q_s, kf_s, vf_s, yc_s, u_s = _in_proj_sample(x2s, gmix, w_in_bf, gq, gk, gsum, conv_w[l], prev0, prev1)
    row_pos = jnp.arange(past * N_HEADS, dtype=jnp.int32) // N_HEADS
    row_head = jnp.arange(past * N_HEADS, dtype=jnp.int32) % N_HEADS
    q_head = jnp.arange(2 * N_HEADS, dtype=jnp.int32) % N_HEADS
    bias = -(slopes[q_head][:, None] * (past - row_pos).astype(F32)[None, :])
    bm = jnp.where(q_head[:, None] == row_head[None, :], bias, NEG_BIG)
    bm = jnp.moveaxis(bm.reshape(2 * N_HEADS, SAMPLE_PARTS, past * N_HEADS // SAMPLE_PARTS), 1, 0)
    o_p, o_s = _attention(page_table, q2, k2, vh, cache_k, cache_v, q_s, kf_s, vf_s, bm, lq1, lk1, lq2, lk2, gsub,
                          layer=l, lam_init=lam_init)
    o_s = o_s.reshape(nb, ATT_W).astype(BF16)

    cnt0 = jnp.zeros((1, LANES), F32)
    tm_p, tm_s = 512, nb
    tri = lambda n: (jnp.arange(n)[:, None] > jnp.arange(n)[None, :]).astype(BF16)
    xm_p, h2_p, info_p, cnt1 = _out_proj(
        o_p.reshape(t_p, ATT_W), yc_p, x2p, w_out_bf, gffn, wr2, br, tri(tm_p), cnt0, tm=tm_p)
    xm_s, h2_s, info_s, cnt2 = _out_proj(
        o_s, yc_s, x2s, w_out_bf, gffn, wr2, br, tri(tm_s), cnt1, tm=tm_s)

    counts = cnt2[0, 0:N_EXPERTS].astype(jnp.int32)
    padded = (counts + MOE_BLK - 1) // MOE_BLK * MOE_BLK
    pad_end = jnp.cumsum(padded)
    pad_start = pad_end - padded
    n_blocks = -(-2 * n_tok // MOE_BLK) + N_EXPERTS
    blk_row = jnp.arange(n_blocks, dtype=jnp.int32) * MOE_BLK
    block_e = jnp.minimum(jnp.sum((pad_end[None, :] <= blk_row[:, None]).astype(jnp.int32), axis=1),
                          N_EXPERTS - 1)
    n_active = (pad_end[-1:] // MOE_BLK).astype(jnp.int32)
    experts = jnp.arange(N_EXPERTS, dtype=jnp.int32)

    def slots(info):
        def one(e_col, r_col):
            e = info[:, e_col].astype(jnp.int32)
            start = jnp.sum(jnp.where(e[:, None] == experts[None, :], pad_start[None, :], 0), axis=1)
            return start + info[:, r_col].astype(jnp.int32)
        return one(INFO_E1, INFO_R1), one(INFO_E2, INFO_R2)

    d1_p, d2_p = slots(info_p)
    d1_s, d2_s = slots(info_s)

    xb = _dispatch(d1_p, d2_p, d1_s, d2_s, pad_start + counts, pad_end, n_active, h2_p, h2_s,
                   n_blocks=n_blocks, tm=512)
    later = (experts[None, :] > experts[:, None]) & (counts[None, :] > 0)
    nxt = jnp.min(jnp.where(later, experts[None, :], N_EXPERTS), axis=1)
    next_expert = jnp.where(nxt < N_EXPERTS, nxt, -1).astype(jnp.int32)
    yb = _experts(block_e, n_active, next_expert, xb, w_gate, w_up, w_down, layer=l)
    y_p = _combine(d1_p, d2_p, xm_p, info_p, yb, tm=128)
    y_s = _combine(d1_s, d2_s, xm_s, info_s, yb, tm=128)

    conv_s = jnp.stack([prev1, u_s], axis=1)
    return (y_p.reshape(batch, seq, D_MODEL), y_s.reshape(nb, 1, D_MODEL),
            kf_p.reshape(batch, seq, N_HEADS, V_DIM), vf_p.reshape(batch, seq, N_HEADS, V_DIM), cs_p,
            kf_s.reshape(nb, 1, N_HEADS, V_DIM), vf_s.reshape(nb, 1, N_HEADS, V_DIM), conv_s)


def kernel(x_prompt, x_sample, cache_k, cache_v, state_conv, page_table, norm_mix, w_in, g_q, g_k,
           lam_q1, lam_k1, lam_q2, lam_k2, g_sub, conv_w, w_out, norm_ffn, w_router_group, b_router_group,
           w_router_expert, b_router_expert, w_gate, w_up, w_down):
    depth = w_in.shape[0]
    xp, xs = x_prompt, x_sample
    outs = [[] for _ in range(6)]
    for l in range(depth):
        xp, xs, *rest = _layer(
            l, xp, xs, cache_k, cache_v, state_conv, page_table, norm_mix, w_in, g_q, g_k,
            lam_q1, lam_k1, lam_q2, lam_k2, g_sub, conv_w, w_out, norm_ffn, w_router_group, b_router_group,
            w_router_expert, b_router_expert, w_gate, w_up, w_down)
        for acc, val in zip(outs, rest):
            acc.append(val)
    return (xp, xs) + tuple(jnp.stack(v, axis=0) for v in outs)
```

```python
import functools
import math

import jax
import jax.numpy as jnp
from jax import lax
from jax.experimental import pallas as pl
from jax.experimental.pallas import tpu as pltpu

D_MODEL = 2048
N_HEADS = 8
V_DIM = 128
QK_HALF = 64
ATT_W = N_HEADS * V_DIM
CONV_W = D_MODEL - ATT_W
CONV_K = 3
IN_COLS = 3 * ATT_W + 3 * CONV_W
N_GROUPS = 4
EPG = 8
N_EXPERTS = N_GROUPS * EPG
D_EXPERT = D_MODEL // 4
PAGE = 128
EPS = 1e-6

LANES = 128
SUBLANES = 8
N_LT = D_MODEL // LANES
MOE_BLK = 256
NEG_BIG = -1e30
POS_SPLIT = 16
ALIBI_LANES = 6
LOG2E = math.log2(math.e)
PAGE_ROWS = PAGE * N_HEADS

F32 = jnp.float32
BF16 = jnp.bfloat16


def _cparams(n_axes, vmem_mb):
    return pltpu.CompilerParams(
        dimension_semantics=("arbitrary",) * n_axes,
        vmem_limit_bytes=vmem_mb * 1024 * 1024)


def _const_spec(shape, single_buffer=False):
    nd = len(shape)
    idx = lambda *_: (0,) * nd
    if single_buffer:
        return pl.BlockSpec(shape, idx, pipeline_mode=pl.Buffered(1))
    return pl.BlockSpec(shape, idx)


def _dot(a, b):
    return jnp.dot(a, b, preferred_element_type=F32)


def _dot_nt(a, b):
    return lax.dot_general(a, b, (((1,), (1,)), ((), ())), preferred_element_type=F32)


def _rms_rows(x, gain):
    ms = jnp.mean(x * x, axis=-1, keepdims=True)
    return x * lax.rsqrt(ms + EPS) * gain


def _diff_lambda(lq1, lk1, lq2, lk2, lam_init):
    s1 = jnp.sum(lq1 * lk1, axis=-1, keepdims=True)
    s2 = jnp.sum(lq2 * lk2, axis=-1, keepdims=True)
    return jnp.exp(s1) - jnp.exp(s2) + lam_init


IN_CHUNK = 512


def _half_norm(p, gsum, gain):
    ms = _dot((p * p).astype(BF16), gsum)
    return p * lax.rsqrt(ms + EPS) * gain


def _in_proj_qkv_kernel(x_ref, gmix_ref, w_ref, gq_ref, gk_ref, gsum_ref, qal_ref,
                        q2_ref, k2_ref, vh_ref, kf_ref, vf_ref, h_scr, *, tm, tiles_per_seq):
    i = pl.program_id(0)
    h_scr[...] = _rms_rows(x_ref[...], gmix_ref[...]).astype(BF16)
    gsum = gsum_ref[...]
    lane = lax.broadcasted_iota(jnp.int32, (tm, LANES), 1)
    first_half = lane < QK_HALF
    heads_per_chunk = IN_CHUNK // V_DIM

    pos = lax.broadcasted_iota(jnp.int32, (tm, LANES), 0) + (i % tiles_per_seq) * tm
    pos_hi = (pos // POS_SPLIT * POS_SPLIT).astype(F32)
    pos_lo = (pos % POS_SPLIT).astype(F32)

    def key_alibi(first_lane):
        rel = lane - first_lane
        return jnp.where((rel >= 0) & (rel < ALIBI_LANES), jnp.where(rel % 2 == 0, pos_hi, pos_lo), 0.0)

    kal_a = key_alibi(QK_HALF)
    kal_b = key_alibi(0)

    for c in range(ATT_W // IN_CHUNK):
        cols = slice(c * IN_CHUNK, (c + 1) * IN_CHUNK)
        p = _dot(h_scr[...], w_ref[:, c * IN_CHUNK:(c + 1) * IN_CHUNK])
        qn = _half_norm(p, gsum, gq_ref[:, cols]) * (LOG2E * QK_HALF ** -0.5)
        for hh in range(heads_per_chunk):
            qh = qn[:, hh * V_DIM:(hh + 1) * V_DIM]
            head = c * heads_per_chunk + hh
            q2_ref[head, 0] = jnp.where(first_half, qh, qal_ref[head, 0:1, :]).astype(BF16)
            q2_ref[head, 1] = jnp.where(first_half, qal_ref[head, 1:2, :], qh).astype(BF16)
        p = _dot(h_scr[...], w_ref[:, ATT_W + c * IN_CHUNK:ATT_W + (c + 1) * IN_CHUNK])
        kn = _half_norm(p, gsum, gk_ref[:, cols])
        kf_ref[:, cols] = kn
        for hh in range(heads_per_chunk):
            kh = kn[:, hh * V_DIM:(hh + 1) * V_DIM]
            head = c * heads_per_chunk + hh
            k2_ref[head, 0] = jnp.where(first_half, kh, kal_a).astype(BF16)
            k2_ref[head, 1] = jnp.where(first_half, kal_b, kh).astype(BF16)
        p = _dot(h_scr[...], w_ref[:, 2 * ATT_W + c * IN_CHUNK:2 * ATT_W + (c + 1) * IN_CHUNK])
        vf_ref[:, cols] = p
        for hh in range(heads_per_chunk):
            vh_ref[c * heads_per_chunk + hh] = p[:, hh * V_DIM:(hh + 1) * V_DIM].astype(BF16)


def _in_proj_conv_kernel(x_ref, gmix_ref, w_ref, cw_ref, yc_ref, cs_ref, h_scr, u_scr, *, tm, tiles_per_seq):
    i = pl.program_id(0)

    @pl.when(i % tiles_per_seq == 0)
    def _():
        u_scr[0:SUBLANES, :] = jnp.zeros((SUBLANES, CONV_W), F32)

    h_scr[...] = _rms_rows(x_ref[...], gmix_ref[...]).astype(BF16)
    for c in range(CONV_W // IN_CHUNK):
        cols = slice(c * IN_CHUNK, (c + 1) * IN_CHUNK)
        lo = c * IN_CHUNK
        bg = _dot(h_scr[...], w_ref[:, lo:lo + IN_CHUNK])
        cg = _dot(h_scr[...], w_ref[:, CONV_W + lo:CONV_W + lo + IN_CHUNK])
        xc = _dot(h_scr[...], w_ref[:, 2 * CONV_W + lo:2 * CONV_W + lo + IN_CHUNK])
        u = cg * xc
        u_scr[SUBLANES:SUBLANES + tm, cols] = u
        u1 = u_scr[SUBLANES - 1:SUBLANES - 1 + tm, cols]
        u2 = u_scr[SUBLANES - 2:SUBLANES - 2 + tm, cols]
        y = cw_ref[0:1, cols] * u2 + cw_ref[1:2, cols] * u1 + cw_ref[2:3, cols] * u
        yc_ref[:, cols] = (bg * y).astype(BF16)

    cs_ref[...] = u_scr[tm + SUBLANES - 2:tm + SUBLANES, :]
    u_scr[0:SUBLANES, :] = u_scr[tm:tm + SUBLANES, :]


def _in_proj_sample_kernel(x_ref, gmix_ref, w_ref, gq_ref, gk_ref, gsum_ref, cw_ref, p0_ref, p1_ref,
                           q_ref, kf_ref, vf_ref, yc_ref, u_ref, h_scr):
    h_scr[...] = _rms_rows(x_ref[...], gmix_ref[...]).astype(BF16)
    gsum = gsum_ref[...]
    heads_per_chunk = IN_CHUNK // V_DIM
    for c in range(ATT_W // IN_CHUNK):
        cols = slice(c * IN_CHUNK, (c + 1) * IN_CHUNK)
        p = _dot(h_scr[...], w_ref[:, c * IN_CHUNK:(c + 1) * IN_CHUNK])
        qn = _half_norm(p, gsum, gq_ref[:, cols]) * (QK_HALF ** -0.5)
        p = _dot(h_scr[...], w_ref[:, ATT_W + c * IN_CHUNK:ATT_W + (c + 1) * IN_CHUNK])
        kn = _half_norm(p, gsum, gk_ref[:, cols])
        vn = _dot(h_scr[...], w_ref[:, 2 * ATT_W + c * IN_CHUNK:2 * ATT_W + (c + 1) * IN_CHUNK])
        for hh in range(heads_per_chunk):
            head = c * heads_per_chunk + hh
            hcols = slice(hh * V_DIM, (hh + 1) * V_DIM)
            q_ref[:, head, :] = qn[:, hcols]
            kf_ref[:, head, :] = kn[:, hcols]
            vf_ref[:, head, :] = vn[:, hcols]
    base = 3 * ATT_W
    for c in range(CONV_W // IN_CHUNK):
        cols = slice(c * IN_CHUNK, (c + 1) * IN_CHUNK)
        lo = c * IN_CHUNK
        bg = _dot(h_scr[...], w_ref[:, base + lo:base + lo + IN_CHUNK])
        cg = _dot(h_scr[...], w_ref[:, base + CONV_W + lo:base + CONV_W + lo + IN_CHUNK])
        xc = _dot(h_scr[...], w_ref[:, base + 2 * CONV_W + lo:base + 2 * CONV_W + lo + IN_CHUNK])
        u = cg * xc
        u_ref[:, cols] = u
        y = cw_ref[0:1, cols] * p0_ref[:, cols] + cw_ref[1:2, cols] * p1_ref[:, cols] + cw_ref[2:3, cols] * u
        yc_ref[:, cols] = (bg * y).astype(BF16)


def _in_proj_prompt(x, gmix, w_bf, gq, gk, gsum, cw, qal, *, batch, seq, tm=512):
    t = batch * seq
    tps = seq // tm
    row = lambda i: (i, 0)
    half_cols = IN_COLS // 2
    x_spec = pl.BlockSpec((tm, D_MODEL), row)
    w_spec = lambda part: pl.BlockSpec((D_MODEL, half_cols), lambda i: (0, part), pipeline_mode=pl.Buffered(1))
    head_blk = lambda i: (i // tps, 0, 0, i % tps, 0)
    q2, k2, vh, kf, vf = pl.pallas_call(
        functools.partial(_in_proj_qkv_kernel, tm=tm, tiles_per_seq=tps),
        out_shape=(
            jax.ShapeDtypeStruct((batch, N_HEADS, 2, seq, V_DIM), BF16),
            jax.ShapeDtypeStruct((batch, N_HEADS, 2, seq, V_DIM), BF16),
            jax.ShapeDtypeStruct((batch, N_HEADS, seq, V_DIM), BF16),
            jax.ShapeDtypeStruct((t, ATT_W), F32),
            jax.ShapeDtypeStruct((t, ATT_W), F32),
        ),
        grid=(t // tm,),
        in_specs=[x_spec, _const_spec((1, D_MODEL)), w_spec(0), _const_spec((1, ATT_W)), _const_spec((1, ATT_W)),
                  _const_spec((IN_CHUNK, IN_CHUNK)), _const_spec((N_HEADS, 2, LANES))],
        out_specs=(
            pl.BlockSpec((None, N_HEADS, 2, tm, V_DIM), head_blk),
            pl.BlockSpec((None, N_HEADS, 2, tm, V_DIM), head_blk),
            pl.BlockSpec((None, N_HEADS, tm, V_DIM), lambda i: (i // tps, 0, i % tps, 0)),
            pl.BlockSpec((tm, ATT_W), row),
            pl.BlockSpec((tm, ATT_W), row),
        ),
        scratch_shapes=[pltpu.VMEM((tm, D_MODEL), BF16)],
        compiler_params=_cparams(1, 56), name="in_proj_qkv",
    )(x, gmix, w_bf, gq, gk, gsum, qal)
    yc, cs = pl.pallas_call(
        functools.partial(_in_proj_conv_kernel, tm=tm, tiles_per_seq=tps),
        out_shape=(
            jax.ShapeDtypeStruct((t, CONV_W), BF16),
            jax.ShapeDtypeStruct((batch, CONV_K - 1, CONV_W), F32),
        ),
        grid=(t // tm,),
        in_specs=[x_spec, _const_spec((1, D_MODEL)), w_spec(1), _const_spec((CONV_K, CONV_W))],
        out_specs=(
            pl.BlockSpec((tm, CONV_W), row),
            pl.BlockSpec((None, CONV_K - 1, CONV_W), lambda i: (i // tps, 0, 0)),
        ),
        scratch_shapes=[pltpu.VMEM((tm, D_MODEL), BF16), pltpu.VMEM((tm + SUBLANES, CONV_W), F32)],
        compiler_params=_cparams(1, 48), name="in_proj_conv",
    )(x, gmix, w_bf, cw)
    return q2, k2, vh, kf, vf, yc, cs


def _in_proj_sample(x, gmix, w_bf, gq, gk, gsum, cw, prev0, prev1):
    t = x.shape[0]
    full = lambda shape: _const_spec(shape)
    out_shape = (
        jax.ShapeDtypeStruct((t, N_HEADS, V_DIM), F32),
        jax.ShapeDtypeStruct((t, N_HEADS, V_DIM), F32),
        jax.ShapeDtypeStruct((t, N_HEADS, V_DIM), F32),
        jax.ShapeDtypeStruct((t, CONV_W), BF16),
        jax.ShapeDtypeStruct((t, CONV_W), F32),
    )
    in_specs = [
        full((t, D_MODEL)), full((1, D_MODEL)), _const_spec((D_MODEL, IN_COLS), single_buffer=True),
        full((1, ATT_W)), full((1, ATT_W)), full((IN_CHUNK, IN_CHUNK)), full((CONV_K, CONV_W)),
        full((t, CONV_W)), full((t, CONV_W)),
    ]
    return pl.pallas_call(
        _in_proj_sample_kernel, out_shape=out_shape, grid=(1,), in_specs=in_specs,
        out_specs=tuple(full(s.shape) for s in out_shape),
        scratch_shapes=[pltpu.VMEM((t, D_MODEL), BF16)],
        compiler_params=_cparams(1, 48), name="in_proj_sample",
    )(x, gmix, w_bf, gq, gk, gsum, cw, prev0, prev1)


def _prompt_query_block(i, lam, q2_ref, k2_ref, v_ref, gsub_ref, o_ref, s_scr, v2_scr, acc_scr, *, tq, lam_init):
    n_lt = tq // LANES
    row = lax.broadcasted_iota(jnp.int32, (tq, tq), 0)
    col = lax.broadcasted_iota(jnp.int32, (tq, tq), 1)
    causal = col <= row

    def lane_tile_max(m_acc, s):
        for t in range(n_lt):
            m_acc = jnp.maximum(m_acc, s[:, t * LANES:(t + 1) * LANES])
        return m_acc

    def lane_tile_sum(l_acc, p):
        for t in range(n_lt):
            l_acc = l_acc + p[:, t * LANES:(t + 1) * LANES]
        return l_acc

    def scores(j, c):
        start = pl.multiple_of(j * tq, tq)
        return _dot_nt(q2_ref[c], k2_ref[c, pl.ds(start, tq), :])

    def pass1(j, carry):
        out = []
        for c in range(2):
            s = scores(j, c)
            s_scr[c, j] = s
            out.append(lane_tile_max(carry[c], s))
        return tuple(out)

    neg = jnp.full((tq, LANES), NEG_BIG, F32)
    m_acc = lax.fori_loop(0, i, pass1, (neg, neg))
    m_rows = []
    for c in range(2):
        s = jnp.where(causal, scores(i, c), NEG_BIG)
        s_scr[c, i] = s
        m_rows.append(jnp.max(lane_tile_max(m_acc[c], s), axis=-1, keepdims=True))

    @pl.when(i == 0)
    def _():
        v2_scr[...] = jnp.zeros(v2_scr.shape, BF16)
        for j in range(v2_scr.shape[0]):
            v = v_ref[j * tq:(j + 1) * tq, :]
            v2_scr[j, 0:tq, 0:V_DIM] = v
            v2_scr[j, tq:2 * tq, V_DIM:2 * V_DIM] = v

    acc_scr[...] = jnp.zeros(acc_scr.shape, F32)

    def pass2(j, carry):
        l0, l1 = carry
        p0 = jnp.exp2(s_scr[0, j] - m_rows[0])
        p1 = jnp.exp2(s_scr[1, j] - m_rows[1])
        pcat = jnp.concatenate([p0.astype(BF16), p1.astype(BF16)], axis=1)
        acc_scr[...] += _dot(pcat, v2_scr[j])
        return lane_tile_sum(l0, p0), lane_tile_sum(l1, p1)

    z = jnp.zeros((tq, LANES), F32)
    l0, l1 = lax.fori_loop(0, i + 1, pass2, (z, z))
    a0 = acc_scr[:, 0:V_DIM]
    a1 = acc_scr[:, V_DIM:2 * V_DIM]
    o = a0 / jnp.sum(l0, axis=-1, keepdims=True) - lam * (a1 / jnp.sum(l1, axis=-1, keepdims=True))
    o_ref[...] = (_rms_rows(o, gsub_ref[...]) * (1.0 - lam_init)).astype(BF16)


SAMPLE_PARTS = 2


def _sample_pages(part, lam, k_refs, v_refs, q_ref, kn_ref, vn_ref, bm_ref, gsub_ref, o_ref,
                  m_scr, l_scr, acc_scr, *, lam_init):
    n_rows = 2 * N_HEADS
    n_pages = len(k_refs)

    @pl.when(part == 0)
    def _():
        m_scr[...] = jnp.full(m_scr.shape, NEG_BIG, F32)
        l_scr[...] = jnp.zeros(l_scr.shape, F32)
        acc_scr[...] = jnp.zeros(acc_scr.shape, F32)

    q8 = q_ref[...]
    lane = lax.broadcasted_iota(jnp.int32, (N_HEADS, V_DIM), 1)
    qm = jnp.concatenate([jnp.where(lane < QK_HALF, q8, 0.0), jnp.where(lane < QK_HALF, 0.0, q8)], axis=0)
    qm_bf = qm.astype(BF16)

    s_parts = [_dot_nt(qm_bf, k_refs[p][...].reshape(PAGE_ROWS, V_DIM).astype(BF16)) for p in range(n_pages)]
    s = jnp.concatenate(s_parts, axis=-1) + bm_ref[...]
    kn = kn_ref[...].astype(BF16).astype(F32)
    s_new = jnp.sum(qm_bf.astype(F32) * jnp.concatenate([kn, kn], axis=0), axis=-1, keepdims=True)
    s_new = jnp.where(part == SAMPLE_PARTS - 1, s_new, NEG_BIG)

    m_prev = m_scr[...]
    m = jnp.maximum(jnp.maximum(jnp.max(s, axis=-1, keepdims=True), s_new), m_prev)
    p = jnp.exp(s - m)
    p_new = jnp.exp(s_new - m)
    keep = jnp.exp(m_prev - m)
    l = keep * l_scr[...] + jnp.sum(p, axis=-1, keepdims=True) + p_new
    p_bf = p.astype(BF16)
    acc = keep * acc_scr[...]
    for pg in range(n_pages):
        acc = acc + _dot(p_bf[:, pg * PAGE_ROWS:(pg + 1) * PAGE_ROWS],
                         v_refs[pg][...].reshape(PAGE_ROWS, V_DIM).astype(BF16))
    vn = vn_ref[...].astype(BF16).astype(F32)
    acc = acc + p_new.astype(BF16).astype(F32) * jnp.concatenate([vn, vn], axis=0)
    m_scr[...] = m
    l_scr[...] = l
    acc_scr[...] = acc

    w = acc / l
    o = w[0:N_HEADS] - lam * w[N_HEADS:n_rows]
    o_ref[...] = _rms_rows(o, gsub_ref[...]) * (1.0 - lam_init)


def _attention_kernel(pt_ref, q2_ref, k2_ref, v_ref, *refs, tq, n_page_refs, lam_init):
    del pt_ref
    k_refs = refs[:n_page_refs]
    v_refs = refs[n_page_refs:2 * n_page_refs]
    (qs_ref, kn_ref, vn_ref, bm_ref, lq1_ref, lk1_ref, lq2_ref, lk2_ref, gsub_ref,
     o_ref, os_ref, s_scr, v2_scr, acc_scr, sm_scr, sl_scr, sacc_scr) = refs[2 * n_page_refs:]
    i = pl.program_id(2)
    lam = _diff_lambda(lq1_ref[...], lk1_ref[...], lq2_ref[...], lk2_ref[...], lam_init)
    _prompt_query_block(i, lam, q2_ref, k2_ref, v_ref, gsub_ref, o_ref, s_scr, v2_scr, acc_scr,
                        tq=tq, lam_init=lam_init)
    _sample_pages(i % SAMPLE_PARTS, lam, k_refs, v_refs, qs_ref, kn_ref, vn_ref, bm_ref, gsub_ref, os_ref,
                  sm_scr, sl_scr, sacc_scr, lam_init=lam_init)


def _attention(page_table, q2, k2, vh, ck, cv, q_s, kn_s, vn_s, bm, lq1, lk1, lq2, lk2, gsub, *,
               layer, lam_init, tq=512):
    batch, _, _, seq, _ = q2.shape
    nq = seq // tq
    nb, n_pages = page_table.shape
    assert nq % SAMPLE_PARTS == 0 and batch * N_HEADS * nq == nb * SAMPLE_PARTS
    assert n_pages % SAMPLE_PARTS == 0
    pps = n_pages // SAMPLE_PARTS
    seqs_per_head = nq // SAMPLE_PARTS
    pt_flat = page_table.reshape(nb * n_pages)

    def sample_seq(b, h, i):
        return (b * N_HEADS + h) * seqs_per_head + i // SAMPLE_PARTS

    def page_spec(p):
        def idx(b, h, i, pt):
            return (layer, pt[sample_seq(b, h, i) * n_pages + (i % SAMPLE_PARTS) * pps + p], 0, 0, 0)
        return pl.BlockSpec((None, None, PAGE, N_HEADS, V_DIM), idx)

    srow = pl.BlockSpec((None, N_HEADS, V_DIM), lambda b, h, i, pt: (sample_seq(b, h, i), 0, 0))
    vec = lambda n: pl.BlockSpec((1, n), lambda b, h, i, pt: (0, 0))
    in_specs = (
        [pl.BlockSpec((None, None, 2, tq, V_DIM), lambda b, h, i, pt: (b, h, 0, i, 0)),
         pl.BlockSpec((None, None, 2, seq, V_DIM), lambda b, h, i, pt: (b, h, 0, 0, 0)),
         pl.BlockSpec((None, None, seq, V_DIM), lambda b, h, i, pt: (b, h, 0, 0))]
        + [page_spec(p) for p in range(pps)] + [page_spec(p) for p in range(pps)]
        + [srow, srow, srow,
           pl.BlockSpec((None, 2 * N_HEADS, pps * PAGE_ROWS), lambda b, h, i, pt: (i % SAMPLE_PARTS, 0, 0)),
           vec(QK_HALF), vec(QK_HALF), vec(QK_HALF), vec(QK_HALF), vec(V_DIM)])
    grid_spec = pltpu.PrefetchScalarGridSpec(
        num_scalar_prefetch=1, grid=(batch, N_HEADS, nq), in_specs=in_specs,
        out_specs=(pl.BlockSpec((None, tq, V_DIM), lambda b, h, i, pt: (b, i, h)), srow),
        scratch_shapes=[pltpu.VMEM((2, nq, tq, tq), F32), pltpu.VMEM((nq, 2 * tq, 2 * V_DIM), BF16),
                        pltpu.VMEM((tq, 2 * V_DIM), F32),
                        pltpu.VMEM((2 * N_HEADS, 1), F32), pltpu.VMEM((2 * N_HEADS, 1), F32),
                        pltpu.VMEM((2 * N_HEADS, V_DIM), F32)])
    return pl.pallas_call(
        functools.partial(_attention_kernel, tq=tq, n_page_refs=pps, lam_init=lam_init),
        out_shape=(jax.ShapeDtypeStruct((batch, seq, ATT_W), BF16),
                   jax.ShapeDtypeStruct((nb, N_HEADS, V_DIM), F32)),
        grid_spec=grid_spec, compiler_params=_cparams(3, 52), name="attention",
    )(pt_flat, q2, k2, vh, *([ck] * pps), *([cv] * pps), q_s, kn_s, vn_s, bm, lq1, lk1, lq2, lk2, gsub)


INFO_E1, INFO_E2, INFO_W1, INFO_W2, INFO_R1, INFO_R2 = 0, 1, 2, 3, 4, 5


def _lane_min_index(mask, lane):
    return jnp.min(jnp.where(mask, lane, float(LANES)), axis=-1, keepdims=True)


def _out_proj_kernel(o_ref, yc_ref, x_ref, w_ref, gffn_ref, wr2_ref, br_ref, tri_ref, cnt_in_ref,
                     xm_ref, h2_ref, info_ref, cnt_ref, cnt_scr, *, tm):
    i = pl.program_id(0)

    @pl.when(i == 0)
    def _():
        cnt_scr[...] = cnt_in_ref[...]

    xm = x_ref[...] + _dot(o_ref[...], w_ref[0:ATT_W, :]) + _dot(yc_ref[...], w_ref[ATT_W:D_MODEL, :])
    xm_ref[...] = xm
    h2 = _rms_rows(xm, gffn_ref[...])
    for s in range(N_LT):
        h2_ref[pl.ds(s, tm, stride=N_LT), :] = h2[:, s * LANES:(s + 1) * LANES]

    hh = h2.astype(BF16)
    hl = (h2 - hh.astype(F32)).astype(BF16)
    parts = _dot(jnp.concatenate([hh, hl], axis=0), wr2_ref[...])
    logits = (parts[0:tm, 0:LANES] + parts[0:tm, LANES:2 * LANES]
              + parts[tm:2 * tm, 0:LANES] + parts[tm:2 * tm, LANES:2 * LANES]) + br_ref[...]

    lane = lax.broadcasted_iota(jnp.int32, (tm, LANES), 1).astype(F32)
    is_grp = lane < N_GROUPS
    lg = jnp.where(is_grp, logits, NEG_BIG)
    mg = jnp.max(lg, axis=-1, keepdims=True)
    g_star = _lane_min_index(is_grp & (lg == mg), lane)
    p_group = 1.0 / jnp.sum(jnp.where(is_grp, jnp.exp(lg - mg), 0.0), axis=-1, keepdims=True)

    e_lo = N_GROUPS + g_star * EPG
    in_grp = (lane >= e_lo) & (lane < e_lo + EPG)
    le = jnp.where(in_grp, logits, NEG_BIG)
    m1 = jnp.max(le, axis=-1, keepdims=True)
    ex = jnp.where(in_grp, jnp.exp(le - m1), 0.0)
    pe = ex / jnp.sum(ex, axis=-1, keepdims=True)
    pe_m = jnp.where(in_grp, pe, -1.0)
    p1 = jnp.max(pe_m, axis=-1, keepdims=True)
    i1 = _lane_min_index(pe_m == p1, lane)
    pe_m2 = jnp.where(lane == i1, -1.0, pe_m)
    p2 = jnp.max(pe_m2, axis=-1, keepdims=True)
    i2 = _lane_min_index(pe_m2 == p2, lane)
    denom = p1 + p2
    w1 = p_group * p1 / denom
    w2 = p_group * p2 / denom
    e1 = i1 - N_GROUPS
    e2 = i2 - N_GROUPS

    oh1 = lane == e1
    oh2 = lane == e2
    oh = jnp.where(oh1 | oh2, 1.0, 0.0)
    prefix = _dot(tri_ref[...], oh.astype(BF16)) + cnt_scr[...]
    r1 = jnp.sum(jnp.where(oh1, prefix, 0.0), axis=-1, keepdims=True)
    r2 = jnp.sum(jnp.where(oh2, prefix, 0.0), axis=-1, keepdims=True)
    cnt_scr[...] = cnt_scr[...] + jnp.sum(oh, axis=0, keepdims=True)
    cnt_ref[...] = cnt_scr[...]

    info = jnp.zeros((tm, LANES), F32)
    for idx, val in ((INFO_E1, e1), (INFO_E2, e2), (INFO_W1, w1), (INFO_W2, w2),
                     (INFO_R1, r1), (INFO_R2, r2)):
        info = jnp.where(lane == idx, val, info)
    info_ref[...] = info


def _out_proj(o, yc, x, w_bf, gffn, wr2, br, tri, cnt_in, *, tm):
    t = x.shape[0]
    row = lambda i: (i, 0)
    in_specs = [
        pl.BlockSpec((tm, ATT_W), row), pl.BlockSpec((tm, CONV_W), row), pl.BlockSpec((tm, D_MODEL), row),
        _const_spec((D_MODEL, D_MODEL), single_buffer=True), _const_spec((1, D_MODEL)),
        _const_spec((D_MODEL, 2 * LANES)), _const_spec((1, LANES)),
        _const_spec((tm, tm)), _const_spec((1, LANES)),
    ]
    out_specs = (
        pl.BlockSpec((tm, D_MODEL), row),
        pl.BlockSpec((tm * N_LT, LANES), row),
        pl.BlockSpec((tm, LANES), row),
        _const_spec((1, LANES)),
    )
    out_shape = (
        jax.ShapeDtypeStruct((t, D_MODEL), F32), jax.ShapeDtypeStruct((t * N_LT, LANES), F32),
        jax.ShapeDtypeStruct((t, LANES), F32), jax.ShapeDtypeStruct((1, LANES), F32),
    )
    return pl.pallas_call(
        functools.partial(_out_proj_kernel, tm=tm),
        out_shape=out_shape, grid=(t // tm,), in_specs=in_specs, out_specs=out_specs,
        scratch_shapes=[pltpu.VMEM((1, LANES), F32)],
        compiler_params=_cparams(1, 56), name="out_proj",
    )(o, yc, x, w_bf, gffn, wr2, br, tri, cnt_in)


def _row_copy(src, dst, sem):
    return pltpu.make_async_copy(src, dst, sem)


ISSUE_UNROLL = 8


def _for_each_row(n, fn):
    def chunk(k, c):
        for u in range(ISSUE_UNROLL):
            fn(k * ISSUE_UNROLL + u)
        return c
    lax.fori_loop(0, n // ISSUE_UNROLL, chunk, 0)


def _rows(ref, first, n=1):
    return ref.at[pl.ds(pl.multiple_of(first * N_LT, N_LT), n * N_LT)]


def _dispatch_kernel(d1_ref, d2_ref, s1_ref, s2_ref, fill_ref, end_ref, na_ref,
                     h2p_ref, h2s_ref, xb_hbm, zero_scr, sem, zsem, *, tm, n_sample, n_blocks):
    step = pl.program_id(0)

    @pl.when(step == 0)
    def _():
        zero_scr[...] = jnp.zeros(zero_scr.shape, F32)

        def pad_rows(fn):
            def per_expert(e, c):
                return lax.fori_loop(fill_ref[e], end_ref[e], lambda r, c2: fn(r, c2), c)
            lax.fori_loop(0, N_EXPERTS, per_expert, 0)

        def tail_blocks(fn):
            lax.fori_loop(na_ref[0], n_blocks, lambda b, c: fn(b, c), 0)

        def start_row(r, c):
            _row_copy(_rows(zero_scr, 0), _rows(xb_hbm, r), zsem).start()
            return c

        def wait_row(r, c):
            _row_copy(_rows(zero_scr, 0), _rows(xb_hbm, 0), zsem).wait()
            return c

        def start_blk(b, c):
            _row_copy(zero_scr, _rows(xb_hbm, b * MOE_BLK, MOE_BLK), zsem).start()
            return c

        def wait_blk(b, c):
            _row_copy(zero_scr, _rows(xb_hbm, 0, MOE_BLK), zsem).wait()
            return c

        pad_rows(start_row)
        tail_blocks(start_blk)

        def sample_row(r, c):
            _row_copy(_rows(h2s_ref, r), _rows(xb_hbm, s1_ref[r]), zsem).start()
            _row_copy(_rows(h2s_ref, r), _rows(xb_hbm, s2_ref[r]), zsem).start()
            return c

        lax.fori_loop(0, n_sample, sample_row, 0)
        pad_rows(wait_row)
        tail_blocks(wait_blk)
        _row_copy(h2s_ref, _rows(xb_hbm, 0, n_sample), zsem).wait()
        _row_copy(h2s_ref, _rows(xb_hbm, 0, n_sample), zsem).wait()

    base = step * tm

    def issue(r):
        t = base + r
        _row_copy(_rows(h2p_ref, r), _rows(xb_hbm, d1_ref[t]), sem).start(priority=0)
        _row_copy(_rows(h2p_ref, r), _rows(xb_hbm, d2_ref[t]), sem).start(priority=1)

    _for_each_row(tm, issue)
    _row_copy(h2p_ref, _rows(xb_hbm, 0, tm), sem).wait()
    _row_copy(h2p_ref, _rows(xb_hbm, 0, tm), sem).wait()


def _dispatch(d1_p, d2_p, d1_s, d2_s, fill, end, n_active, h2_p, h2_s, *, n_blocks, tm):
    n_p = h2_p.shape[0] // N_LT
    n_s = h2_s.shape[0] // N_LT
    grid_spec = pltpu.PrefetchScalarGridSpec(
        num_scalar_prefetch=7, grid=(n_p // tm,),
        in_specs=[pl.BlockSpec((tm * N_LT, LANES), lambda i, *_: (i, 0)),
                  pl.BlockSpec((n_s * N_LT, LANES), lambda i, *_: (0, 0))],
        out_specs=pl.BlockSpec(memory_space=pl.ANY),
        scratch_shapes=[pltpu.VMEM((MOE_BLK * N_LT, LANES), F32),
                        pltpu.SemaphoreType.DMA(()), pltpu.SemaphoreType.DMA(())])
    return pl.pallas_call(
        functools.partial(_dispatch_kernel, tm=tm, n_sample=n_s, n_blocks=n_blocks),
        out_shape=jax.ShapeDtypeStruct((n_blocks * MOE_BLK * N_LT, LANES), F32), grid_spec=grid_spec,
        compiler_params=_cparams(1, 32), name="moe_dispatch",
    )(d1_p, d2_p, d1_s, d2_s, fill, end, n_active, h2_p, h2_s)


def _experts_kernel(be_ref, na_ref, nxt_ref, xb_ref, wg_hbm, wu_hbm, wd_hbm, yb_ref,
                    x_scr, wg_f32, wu_f32, wd_f32, wg_scr, wu_scr, wd_scr, slot_ref, sems, *, layer):
    b = pl.program_id(0)
    active = b < na_ref[0]
    expert = be_ref[b]
    new_expert = (b == 0) | (expert != be_ref[jnp.maximum(b - 1, 0)])

    def weight_copies(e, slot):
        return [pltpu.make_async_copy(src.at[layer, e], dst.at[slot], sems.at[slot])
                for src, dst in ((wg_hbm, wg_f32), (wu_hbm, wu_f32), (wd_hbm, wd_f32))]

    @pl.when(b == 0)
    def _():
        slot_ref[0] = 0
        for cp in weight_copies(expert, 0):
            cp.start()

    @pl.when(active & new_expert)
    def _():
        slot = slot_ref[0]
        for cp in weight_copies(expert, slot):
            cp.wait()
        wg_scr[...] = wg_f32[slot].astype(BF16)
        wu_scr[...] = wu_f32[slot].astype(BF16)
        wd_scr[...] = wd_f32[slot].astype(BF16)
        nxt = nxt_ref[expert]

        @pl.when(nxt >= 0)
        def _():
            for cp in weight_copies(nxt, 1 - slot):
                cp.start()

        slot_ref[0] = 1 - slot

    @pl.when(active)
    def _():
        for s in range(N_LT):
            x_scr[:, s * LANES:(s + 1) * LANES] = xb_ref[pl.ds(s, MOE_BLK, stride=N_LT), :]
        x = x_scr[...].astype(BF16)
        g = _dot(x, wg_scr[...])
        u = _dot(x, wu_scr[...])
        a = (g * (1.0 / (1.0 + jnp.exp(-g))) * u).astype(BF16)
        y = _dot(a, wd_scr[...])
        for s in range(N_LT):
            yb_ref[pl.ds(s, MOE_BLK, stride=N_LT), :] = y[:, s * LANES:(s + 1) * LANES]

    @pl.when(pl.program_id(0) >= na_ref[0])
    def _():
        yb_ref[...] = jnp.zeros(yb_ref.shape, F32)


def _experts(block_e, n_active, next_expert, xb, wg, wu, wd, *, layer):
    n_slots = xb.shape[0] // N_LT
    n_blocks = n_slots // MOE_BLK
    blk = lambda b, be, na, nx: (jnp.minimum(b, na[0] - 1), 0)
    grid_spec = pltpu.PrefetchScalarGridSpec(
        num_scalar_prefetch=3, grid=(n_blocks,),
        in_specs=[
            pl.BlockSpec((MOE_BLK * N_LT, LANES), blk),
            pl.BlockSpec(memory_space=pl.ANY), pl.BlockSpec(memory_space=pl.ANY), pl.BlockSpec(memory_space=pl.ANY),
        ],
        out_specs=pl.BlockSpec((MOE_BLK * N_LT, LANES), lambda b, be, na, nx: (b, 0)),
        scratch_shapes=[pltpu.VMEM((MOE_BLK, D_MODEL), F32),
                        pltpu.VMEM((2, D_MODEL, D_EXPERT), F32), pltpu.VMEM((2, D_MODEL, D_EXPERT), F32),
                        pltpu.VMEM((2, D_EXPERT, D_MODEL), F32),
                        pltpu.VMEM((D_MODEL, D_EXPERT), BF16), pltpu.VMEM((D_MODEL, D_EXPERT), BF16),
                        pltpu.VMEM((D_EXPERT, D_MODEL), BF16),
                        pltpu.SMEM((1,), jnp.int32), pltpu.SemaphoreType.DMA((2,))])
    return pl.pallas_call(
        functools.partial(_experts_kernel, layer=layer),
        out_shape=jax.ShapeDtypeStruct(xb.shape, F32), grid_spec=grid_spec,
        compiler_params=_cparams(1, 52), name="moe_experts",
    )(block_e, n_active, next_expert, xb, wg, wu, wd)


def _combine_kernel(d1_ref, d2_ref, xm_ref, info_ref, yb_hbm, out_ref, a_scr, b_scr, sems, *, tm, n_tiles):
    i = pl.program_id(0)

    def fetch(tile, slot):
        base = tile * tm

        def issue(r):
            t = base + r
            _row_copy(_rows(yb_hbm, d1_ref[t]), _rows(a_scr.at[slot], r), sems.at[slot]).start(priority=0)
            _row_copy(_rows(yb_hbm, d2_ref[t]), _rows(b_scr.at[slot], r), sems.at[slot]).start(priority=1)

        _for_each_row(tm, issue)

    @pl.when(i == 0)
    def _():
        fetch(0, 0)

    @pl.when(i + 1 < n_tiles)
    def _():
        fetch(i + 1, (i + 1) % 2)

    slot = i % 2
    _row_copy(_rows(yb_hbm, 0, tm), a_scr.at[slot], sems.at[slot]).wait()
    _row_copy(_rows(yb_hbm, 0, tm), b_scr.at[slot], sems.at[slot]).wait()

    info = info_ref[...]
    w1 = info[:, INFO_W1:INFO_W1 + 1]
    w2 = info[:, INFO_W2:INFO_W2 + 1]
    for s in range(N_LT):
        cols = slice(s * LANES, (s + 1) * LANES)
        lane_tile = pl.ds(s, tm, stride=N_LT)
        out_ref[:, cols] = xm_ref[:, cols] + (a_scr[slot, lane_tile, :] * w1 + b_scr[slot, lane_tile, :] * w2)


def _combine(dest1, dest2, xm_buf, info_buf, yb, *, tm):
    n_rows = xm_buf.shape[0]
    n_tiles = n_rows // tm
    grid_spec = pltpu.PrefetchScalarGridSpec(
        num_scalar_prefetch=2, grid=(n_tiles,),
        in_specs=[
            pl.BlockSpec((tm, D_MODEL), lambda i, a, b: (i, 0)),
            pl.BlockSpec((tm, LANES), lambda i, a, b: (i, 0)),
            pl.BlockSpec(memory_space=pl.ANY),
        ],
        out_specs=pl.BlockSpec((tm, D_MODEL), lambda i, a, b: (i, 0)),
        scratch_shapes=[pltpu.VMEM((2, tm * N_LT, LANES), F32), pltpu.VMEM((2, tm * N_LT, LANES), F32),
                        pltpu.SemaphoreType.DMA((2,))])
    return pl.pallas_call(
        functools.partial(_combine_kernel, tm=tm, n_tiles=n_tiles),
        out_shape=jax.ShapeDtypeStruct((n_rows, D_MODEL), F32), grid_spec=grid_spec,
        compiler_params=_cparams(1, 32), name="moe_combine",
    )(dest1, dest2, xm_buf, info_buf, yb)


def _layer(l, xp, xs, cache_k, cache_v, state_conv, page_table, norm_mix, w_in, g_q, g_k,
           lam_q1, lam_k1, lam_q2, lam_k2, g_sub, conv_w, w_out, norm_ffn, w_rg, b_rg, w_re, b_re,
           w_gate, w_up, w_down):
    batch, seq, _ = xp.shape
    nb = xs.shape[0]
    n_pages = page_table.shape[1]
    past = n_pages * PAGE
    t_p = batch * seq
    n_tok = t_p + nb
    lam_init = 0.8 - 0.6 * math.exp(-0.3 * l)

    w_in_bf = w_in[l].astype(BF16)
    w_out_bf = w_out[l].astype(BF16)
    gmix = norm_mix[l].reshape(1, D_MODEL)
    gffn = norm_ffn[l].reshape(1, D_MODEL)
    gq = jnp.tile(g_q[l], ATT_W // QK_HALF).reshape(1, ATT_W)
    gk = jnp.tile(g_k[l], ATT_W // QK_HALF).reshape(1, ATT_W)
    gsub = g_sub[l].reshape(1, V_DIM)
    grp = jnp.arange(IN_CHUNK) // QK_HALF
    gsum = jnp.where(grp[:, None] == grp[None, :], 1.0 / QK_HALF, 0.0).astype(BF16)
    lq1, lk1, lq2, lk2 = (v[l].reshape(1, QK_HALF) for v in (lam_q1, lam_k1, lam_q2, lam_k2))
    slopes = 2.0 ** (-8.0 * jnp.arange(1, N_HEADS + 1, dtype=F32) / N_HEADS)
    lane = jnp.arange(LANES)
    sl = slopes * LOG2E
    s_hi = sl.astype(BF16).astype(F32)
    s_mid = (sl - s_hi).astype(BF16).astype(F32)
    s_lo = (sl - s_hi - s_mid).astype(BF16).astype(F32)
    terms = jnp.stack([s_hi, s_hi, s_mid, s_mid, s_lo, s_lo], axis=1)

    def query_alibi(first_lane):
        rel = lane - first_lane
        return jnp.where((rel >= 0) & (rel < ALIBI_LANES), terms[:, jnp.clip(rel, 0, ALIBI_LANES - 1)], 0.0)

    qal = jnp.stack([query_alibi(QK_HALF), query_alibi(0)], axis=1)
    w_r = jnp.zeros((D_MODEL, LANES), F32)
    w_r = w_r.at[:, 0:N_GROUPS].set(w_rg[l]).at[:, N_GROUPS:N_GROUPS + N_EXPERTS].set(w_re[l])
    wrh = w_r.astype(BF16)
    wr2 = jnp.concatenate([wrh, (w_r - wrh.astype(F32)).astype(BF16)], axis=1)
    br = jnp.zeros((1, LANES), F32)
    br = br.at[0, 0:N_GROUPS].set(b_rg[l]).at[0, N_GROUPS:N_GROUPS + N_EXPERTS].set(b_re[l])

    x2p = xp.reshape(t_p, D_MODEL)
    x2s = xs.reshape(nb, D_MODEL)
    q2, k2, vh, kf_p, vf_p, yc_p, cs_p = _in_proj_prompt(
        x2p, gmix, w_in_bf, gq, gk, gsum, conv_w[l], qal, batch=batch, seq=seq)

    prev0 = state_conv[l, :, 0, :]
    prev1 = state_conv[l, :, 1, :]
    q_s, kf_s, vf_s, yc_s, u_s = _in_proj_sample(x2s, gmix, w_in_bf, gq, gk, gsum, conv_w[l], prev0, prev1)
    row_pos = jnp.arange(past * N_HEADS, dtype=jnp.int32) // N_HEADS
    row_head = jnp.arange(past * N_HEADS, dtype=jnp.int32) % N_HEADS
    q_head = jnp.arange(2 * N_HEADS, dtype=jnp.int32) % N_HEADS
    bias = -(slopes[q_head][:, None] * (past - row_pos).astype(F32)[None, :])
    bm = jnp.where(q_head[:, None] == row_head[None, :], bias, NEG_BIG)
    bm = jnp.moveaxis(bm.reshape(2 * N_HEADS, SAMPLE_PARTS, past * N_HEADS // SAMPLE_PARTS), 1, 0)
    o_p, o_s = _attention(page_table, q2, k2, vh, cache_k, cache_v, q_s, kf_s, vf_s, bm, lq1, lk1, lq2, lk2, gsub,
                          layer=l, lam_init=lam_init)
    o_s = o_s.reshape(nb, ATT_W).astype(BF16)

    cnt0 = jnp.zeros((1, LANES), F32)
    tm_p, tm_s = 512, nb
    tri = lambda n: (jnp.arange(n)[:, None] > jnp.arange(n)[None, :]).astype(BF16)
    xm_p, h2_p, info_p, cnt1 = _out_proj(
        o_p.reshape(t_p, ATT_W), yc_p, x2p, w_out_bf, gffn, wr2, br, tri(tm_p), cnt0, tm=tm_p)
    xm_s, h2_s, info_s, cnt2 = _out_proj(
        o_s, yc_s, x2s, w_out_bf, gffn, wr2, br, tri(tm_s), cnt1, tm=tm_s)

    counts = cnt2[0, 0:N_EXPERTS].astype(jnp.int32)
    padded = (counts + MOE_BLK - 1) // MOE_BLK * MOE_BLK
    pad_end = jnp.cumsum(padded)
    pad_start = pad_end - padded
    n_blocks = -(-2 * n_tok // MOE_BLK) + N_EXPERTS
    blk_row = jnp.arange(n_blocks, dtype=jnp.int32) * MOE_BLK
    block_e = jnp.minimum(jnp.sum((pad_end[None, :] <= blk_row[:, None]).astype(jnp.int32), axis=1),
                          N_EXPERTS - 1)
    n_active = (pad_end[-1:] // MOE_BLK).astype(jnp.int32)
    experts = jnp.arange(N_EXPERTS, dtype=jnp.int32)

    def slots(info):
        def one(e_col, r_col):
            e = info[:, e_col].astype(jnp.int32)
            start = jnp.sum(jnp.where(e[:, None] == experts[None, :], pad_start[None, :], 0), axis=1)
            return start + info[:, r_col].astype(jnp.int32)
        return one(INFO_E1, INFO_R1), one(INFO_E2, INFO_R2)

    d1_p, d2_p = slots(info_p)
    d1_s, d2_s = slots(info_s)

    xb = _dispatch(d1_p, d2_p, d1_s, d2_s, pad_start + counts, pad_end, n_active, h2_p, h2_s,
                   n_blocks=n_blocks, tm=512)
    later = (experts[None, :] > experts[:, None]) & (counts[None, :] > 0)
    nxt = jnp.min(jnp.where(later, experts[None, :], N_EXPERTS), axis=1)
    next_expert = jnp.where(nxt < N_EXPERTS, nxt, -1).astype(jnp.int32)
    yb = _experts(block_e, n_active, next_expert, xb, w_gate, w_up, w_down, layer=l)
    y_p = _combine(d1_p, d2_p, xm_p, info_p, yb, tm=256)
    y_s = _combine(d1_s, d2_s, xm_s, info_s, yb, tm=128)

    conv_s = jnp.stack([prev1, u_s], axis=1)
    return (y_p.reshape(batch, seq, D_MODEL), y_s.reshape(nb, 1, D_MODEL),
            kf_p.reshape(batch, seq, N_HEADS, V_DIM), vf_p.reshape(batch, seq, N_HEADS, V_DIM), cs_p,
            kf_s.reshape(nb, 1, N_HEADS, V_DIM), vf_s.reshape(nb, 1, N_HEADS, V_DIM), conv_s)


def kernel(x_prompt, x_sample, cache_k, cache_v, state_conv, page_table, norm_mix, w_in, g_q, g_k,
           lam_q1, lam_k1, lam_q2, lam_k2, g_sub, conv_w, w_out, norm_ffn, w_router_group, b_router_group,
           w_router_expert, b_router_expert, w_gate, w_up, w_down):
    depth = w_in.shape[0]
    xp, xs = x_prompt, x_sample
    outs = [[] for _ in range(6)]
    for l in range(depth):
        xp, xs, *rest = _layer(
            l, xp, xs, cache_k, cache_v, state_conv, page_table, norm_mix, w_in, g_q, g_k,
            lam_q1, lam_k1, lam_q2, lam_k2, g_sub, conv_w, w_out, norm_ffn, w_router_group, b_router_group,
            w_router_expert, b_router_expert, w_gate, w_up, w_down)
        for acc, val in zip(outs, rest):
            acc.append(val)
    return (xp, xs) + tuple(jnp.stack(v, axis=0) for v in outs)
```

```python
import functools
import math

import jax
import jax.numpy as jnp
from jax import lax
from jax.experimental import pallas as pl
from jax.experimental.pallas import tpu as pltpu

D_MODEL = 2048
N_HEADS = 8
V_DIM = 128
QK_HALF = 64
ATT_W = N_HEADS * V_DIM
CONV_W = D_MODEL - ATT_W
CONV_K = 3
IN_COLS = 3 * ATT_W + 3 * CONV_W
N_GROUPS = 4
EPG = 8
N_EXPERTS = N_GROUPS * EPG
D_EXPERT = D_MODEL // 4
PAGE = 128
EPS = 1e-6

LANES = 128
SUBLANES = 8
MOE_BLK = 256
NEG_BIG = -1e30
POS_SPLIT = 16
ALIBI_LANES = 6
LOG2E = math.log2(math.e)
PAGE_ROWS = PAGE * N_HEADS

F32 = jnp.float32
BF16 = jnp.bfloat16


def _cparams(n_axes, vmem_mb):
    return pltpu.CompilerParams(
        dimension_semantics=("arbitrary",) * n_axes,
        vmem_limit_bytes=vmem_mb * 1024 * 1024)


def _const_spec(shape, single_buffer=False):
    nd = len(shape)
    idx = lambda *_: (0,) * nd
    if single_buffer:
        return pl.BlockSpec(shape, idx, pipeline_mode=pl.Buffered(1))
    return pl.BlockSpec(shape, idx)


def _dot(a, b):
    return jnp.dot(a, b, preferred_element_type=F32)


def _dot_nt(a, b):
    return lax.dot_general(a, b, (((1,), (1,)), ((), ())), preferred_element_type=F32)


def _rms_rows(x, gain):
    ms = jnp.mean(x * x, axis=-1, keepdims=True)
    return x * lax.rsqrt(ms + EPS) * gain


def _diff_lambda(lq1, lk1, lq2, lk2, lam_init):
    s1 = jnp.sum(lq1 * lk1, axis=-1, keepdims=True)
    s2 = jnp.sum(lq2 * lk2, axis=-1, keepdims=True)
    return jnp.exp(s1) - jnp.exp(s2) + lam_init


IN_CHUNK = 512


def _half_norm(p, gsum, gain):
    ms = _dot((p * p).astype(BF16), gsum)
    return p * lax.rsqrt(ms + EPS) * gain


def _in_proj_qkv_kernel(x_ref, gmix_ref, w_ref, gq_ref, gk_ref, gsum_ref, qal_ref,
                        q2_ref, k2_ref, vh_ref, kf_ref, vf_ref, h_scr, *, tm, tiles_per_seq):
    i = pl.program_id(0)
    h_scr[...] = _rms_rows(x_ref[...], gmix_ref[...]).astype(BF16)
    gsum = gsum_ref[...]
    lane = lax.broadcasted_iota(jnp.int32, (tm, LANES), 1)
    first_half = lane < QK_HALF
    heads_per_chunk = IN_CHUNK // V_DIM

    pos = lax.broadcasted_iota(jnp.int32, (tm, LANES), 0) + (i % tiles_per_seq) * tm
    pos_hi = (pos // POS_SPLIT * POS_SPLIT).astype(F32)
    pos_lo = (pos % POS_SPLIT).astype(F32)

    def key_alibi(first_lane):
        rel = lane - first_lane
        return jnp.where((rel >= 0) & (rel < ALIBI_LANES), jnp.where(rel % 2 == 0, pos_hi, pos_lo), 0.0)

    kal_a = key_alibi(QK_HALF)
    kal_b = key_alibi(0)

    for c in range(ATT_W // IN_CHUNK):
        cols = slice(c * IN_CHUNK, (c + 1) * IN_CHUNK)
        p = _dot(h_scr[...], w_ref[:, c * IN_CHUNK:(c + 1) * IN_CHUNK])
        qn = _half_norm(p, gsum, gq_ref[:, cols]) * (LOG2E * QK_HALF ** -0.5)
        for hh in range(heads_per_chunk):
            qh = qn[:, hh * V_DIM:(hh + 1) * V_DIM]
            head = c * heads_per_chunk + hh
            q2_ref[head, 0] = jnp.where(first_half, qh, qal_ref[head, 0:1, :]).astype(BF16)
            q2_ref[head, 1] = jnp.where(first_half, qal_ref[head, 1:2, :], qh).astype(BF16)
        p = _dot(h_scr[...], w_ref[:, ATT_W + c * IN_CHUNK:ATT_W + (c + 1) * IN_CHUNK])
        kn = _half_norm(p, gsum, gk_ref[:, cols])
        kf_ref[:, cols] = kn
        for hh in range(heads_per_chunk):
            kh = kn[:, hh * V_DIM:(hh + 1) * V_DIM]
            head = c * heads_per_chunk + hh
            k2_ref[head, 0] = jnp.where(first_half, kh, kal_a).astype(BF16)
            k2_ref[head, 1] = jnp.where(first_half, kal_b, kh).astype(BF16)
        p = _dot(h_scr[...], w_ref[:, 2 * ATT_W + c * IN_CHUNK:2 * ATT_W + (c + 1) * IN_CHUNK])
        vf_ref[:, cols] = p
        for hh in range(heads_per_chunk):
            vh_ref[c * heads_per_chunk + hh] = p[:, hh * V_DIM:(hh + 1) * V_DIM].astype(BF16)


def _in_proj_conv_kernel(x_ref, gmix_ref, w_ref, cw_ref, yc_ref, cs_ref, h_scr, u_scr, *, tm, tiles_per_seq):
    i = pl.program_id(0)

    @pl.when(i % tiles_per_seq == 0)
    def _():
        u_scr[0:SUBLANES, :] = jnp.zeros((SUBLANES, CONV_W), F32)

    h_scr[...] = _rms_rows(x_ref[...], gmix_ref[...]).astype(BF16)
    for c in range(CONV_W // IN_CHUNK):
        cols = slice(c * IN_CHUNK, (c + 1) * IN_CHUNK)
        lo = c * IN_CHUNK
        bg = _dot(h_scr[...], w_ref[:, lo:lo + IN_CHUNK])
        cg = _dot(h_scr[...], w_ref[:, CONV_W + lo:CONV_W + lo + IN_CHUNK])
        xc = _dot(h_scr[...], w_ref[:, 2 * CONV_W + lo:2 * CONV_W + lo + IN_CHUNK])
        u = cg * xc
        u_scr[SUBLANES:SUBLANES + tm, cols] = u
        u1 = u_scr[SUBLANES - 1:SUBLANES - 1 + tm, cols]
        u2 = u_scr[SUBLANES - 2:SUBLANES - 2 + tm, cols]
        y = cw_ref[0:1, cols] * u2 + cw_ref[1:2, cols] * u1 + cw_ref[2:3, cols] * u
        yc_ref[:, cols] = (bg * y).astype(BF16)

    cs_ref[...] = u_scr[tm + SUBLANES - 2:tm + SUBLANES, :]
    u_scr[0:SUBLANES, :] = u_scr[tm:tm + SUBLANES, :]


def _in_proj_sample_kernel(x_ref, gmix_ref, w_ref, gq_ref, gk_ref, gsum_ref, cw_ref, p0_ref, p1_ref,
                           q_ref, kf_ref, vf_ref, yc_ref, u_ref, h_scr):
    h_scr[...] = _rms_rows(x_ref[...], gmix_ref[...]).astype(BF16)
    gsum = gsum_ref[...]
    heads_per_chunk = IN_CHUNK // V_DIM
    for c in range(ATT_W // IN_CHUNK):
        cols = slice(c * IN_CHUNK, (c + 1) * IN_CHUNK)
        p = _dot(h_scr[...], w_ref[:, c * IN_CHUNK:(c + 1) * IN_CHUNK])
        qn = _half_norm(p, gsum, gq_ref[:, cols]) * (QK_HALF ** -0.5)
        p = _dot(h_scr[...], w_ref[:, ATT_W + c * IN_CHUNK:ATT_W + (c + 1) * IN_CHUNK])
        kn = _half_norm(p, gsum, gk_ref[:, cols])
        vn = _dot(h_scr[...], w_ref[:, 2 * ATT_W + c * IN_CHUNK:2 * ATT_W + (c + 1) * IN_CHUNK])
        for hh in range(heads_per_chunk):
            head = c * heads_per_chunk + hh
            hcols = slice(hh * V_DIM, (hh + 1) * V_DIM)
            q_ref[:, head, :] = qn[:, hcols]
            kf_ref[:, head, :] = kn[:, hcols]
            vf_ref[:, head, :] = vn[:, hcols]
    base = 3 * ATT_W
    for c in range(CONV_W // IN_CHUNK):
        cols = slice(c * IN_CHUNK, (c + 1) * IN_CHUNK)
        lo = c * IN_CHUNK
        bg = _dot(h_scr[...], w_ref[:, base + lo:base + lo + IN_CHUNK])
        cg = _dot(h_scr[...], w_ref[:, base + CONV_W + lo:base + CONV_W + lo + IN_CHUNK])
        xc = _dot(h_scr[...], w_ref[:, base + 2 * CONV_W + lo:base + 2 * CONV_W + lo + IN_CHUNK])
        u = cg * xc
        u_ref[:, cols] = u
        y = cw_ref[0:1, cols] * p0_ref[:, cols] + cw_ref[1:2, cols] * p1_ref[:, cols] + cw_ref[2:3, cols] * u
        yc_ref[:, cols] = (bg * y).astype(BF16)


def _in_proj_prompt(x, gmix, w_bf, gq, gk, gsum, cw, qal, *, batch, seq, tm=512):
    t = batch * seq
    tps = seq // tm
    row = lambda i: (i, 0)
    half_cols = IN_COLS // 2
    x_spec = pl.BlockSpec((tm, D_MODEL), row)
    w_spec = lambda part: pl.BlockSpec((D_MODEL, half_cols), lambda i: (0, part), pipeline_mode=pl.Buffered(1))
    head_blk = lambda i: (i // tps, 0, 0, i % tps, 0)
    q2, k2, vh, kf, vf = pl.pallas_call(
        functools.partial(_in_proj_qkv_kernel, tm=tm, tiles_per_seq=tps),
        out_shape=(
            jax.ShapeDtypeStruct((batch, N_HEADS, 2, seq, V_DIM), BF16),
            jax.ShapeDtypeStruct((batch, N_HEADS, 2, seq, V_DIM), BF16),
            jax.ShapeDtypeStruct((batch, N_HEADS, seq, V_DIM), BF16),
            jax.ShapeDtypeStruct((t, ATT_W), F32),
            jax.ShapeDtypeStruct((t, ATT_W), F32),
        ),
        grid=(t // tm,),
        in_specs=[x_spec, _const_spec((1, D_MODEL)), w_spec(0), _const_spec((1, ATT_W)), _const_spec((1, ATT_W)),
                  _const_spec((IN_CHUNK, IN_CHUNK)), _const_spec((N_HEADS, 2, LANES))],
        out_specs=(
            pl.BlockSpec((None, N_HEADS, 2, tm, V_DIM), head_blk),
            pl.BlockSpec((None, N_HEADS, 2, tm, V_DIM), head_blk),
            pl.BlockSpec((None, N_HEADS, tm, V_DIM), lambda i: (i // tps, 0, i % tps, 0)),
            pl.BlockSpec((tm, ATT_W), row),
            pl.BlockSpec((tm, ATT_W), row),
        ),
        scratch_shapes=[pltpu.VMEM((tm, D_MODEL), BF16)],
        compiler_params=_cparams(1, 56), name="in_proj_qkv",
    )(x, gmix, w_bf, gq, gk, gsum, qal)
    yc, cs = pl.pallas_call(
        functools.partial(_in_proj_conv_kernel, tm=tm, tiles_per_seq=tps),
        out_shape=(
            jax.ShapeDtypeStruct((t, CONV_W), BF16),
            jax.ShapeDtypeStruct((batch, CONV_K - 1, CONV_W), F32),
        ),
        grid=(t // tm,),
        in_specs=[x_spec, _const_spec((1, D_MODEL)), w_spec(1), _const_spec((CONV_K, CONV_W))],
        out_specs=(
            pl.BlockSpec((tm, CONV_W), row),
            pl.BlockSpec((None, CONV_K - 1, CONV_W), lambda i: (i // tps, 0, 0)),
        ),
        scratch_shapes=[pltpu.VMEM((tm, D_MODEL), BF16), pltpu.VMEM((tm + SUBLANES, CONV_W), F32)],
        compiler_params=_cparams(1, 48), name="in_proj_conv",
    )(x, gmix, w_bf, cw)
    return q2, k2, vh, kf, vf, yc, cs


def _in_proj_sample(x, gmix, w_bf, gq, gk, gsum, cw, prev0, prev1):
    t = x.shape[0]
    full = lambda shape: _const_spec(shape)
    out_shape = (
        jax.ShapeDtypeStruct((t, N_HEADS, V_DIM), F32),
        jax.ShapeDtypeStruct((t, N_HEADS, V_DIM), F32),
        jax.ShapeDtypeStruct((t, N_HEADS, V_DIM), F32),
        jax.ShapeDtypeStruct((t, CONV_W), BF16),
        jax.ShapeDtypeStruct((t, CONV_W), F32),
    )
    in_specs = [
        full((t, D_MODEL)), full((1, D_MODEL)), _const_spec((D_MODEL, IN_COLS), single_buffer=True),
        full((1, ATT_W)), full((1, ATT_W)), full((IN_CHUNK, IN_CHUNK)), full((CONV_K, CONV_W)),
        full((t, CONV_W)), full((t, CONV_W)),
    ]
    return pl.pallas_call(
        _in_proj_sample_kernel, out_shape=out_shape, grid=(1,), in_specs=in_specs,
        out_specs=tuple(full(s.shape) for s in out_shape),
        scratch_shapes=[pltpu.VMEM((t, D_MODEL), BF16)],
        compiler_params=_cparams(1, 48), name="in_proj_sample",
    )(x, gmix, w_bf, gq, gk, gsum, cw, prev0, prev1)


def _prompt_query_block(i, first_step, lam, q2_ref, k2_ref, v_ref, gsub_ref, o_ref, s_scr, v2_scr, acc_scr, *,
                        tq, lam_init):
    n_lt = tq // LANES
    row = lax.broadcasted_iota(jnp.int32, (tq, tq), 0)
    col = lax.broadcasted_iota(jnp.int32, (tq, tq), 1)
    causal = col <= row

    def lane_tile_max(m_acc, s):
        for t in range(n_lt):
            m_acc = jnp.maximum(m_acc, s[:, t * LANES:(t + 1) * LANES])
        return m_acc

    def lane_tile_sum(l_acc, p):
        for t in range(n_lt):
            l_acc = l_acc + p[:, t * LANES:(t + 1) * LANES]
        return l_acc

    def scores(j, c):
        start = pl.multiple_of(j * tq, tq)
        return _dot_nt(q2_ref[c], k2_ref[c, pl.ds(start, tq), :])

    def pass1(j, carry):
        out = []
        for c in range(2):
            s = scores(j, c)
            s_scr[c, j] = s
            out.append(lane_tile_max(carry[c], s))
        return tuple(out)

    neg = jnp.full((tq, LANES), NEG_BIG, F32)
    m_acc = lax.fori_loop(0, i, pass1, (neg, neg))
    m_rows = []
    for c in range(2):
        s = jnp.where(causal, scores(i, c), NEG_BIG)
        s_scr[c, i] = s
        m_rows.append(jnp.max(lane_tile_max(m_acc[c], s), axis=-1, keepdims=True))

    @pl.when(first_step)
    def _():
        v2_scr[...] = jnp.zeros(v2_scr.shape, BF16)

    @pl.when(i == 0)
    def _():
        for j in range(v2_scr.shape[0]):
            v = v_ref[j * tq:(j + 1) * tq, :]
            v2_scr[j, 0:tq, 0:V_DIM] = v
            v2_scr[j, tq:2 * tq, V_DIM:2 * V_DIM] = v

    acc_scr[...] = jnp.zeros(acc_scr.shape, F32)

    def pass2(j, carry):
        l0, l1 = carry
        p0 = jnp.exp2(s_scr[0, j] - m_rows[0])
        p1 = jnp.exp2(s_scr[1, j] - m_rows[1])
        pcat = jnp.concatenate([p0.astype(BF16), p1.astype(BF16)], axis=1)
        acc_scr[...] += _dot(pcat, v2_scr[j])
        return lane_tile_sum(l0, p0), lane_tile_sum(l1, p1)

    z = jnp.zeros((tq, LANES), F32)
    l0, l1 = lax.fori_loop(0, i + 1, pass2, (z, z))
    a0 = acc_scr[:, 0:V_DIM]
    a1 = acc_scr[:, V_DIM:2 * V_DIM]
    o = a0 / jnp.sum(l0, axis=-1, keepdims=True) - lam * (a1 / jnp.sum(l1, axis=-1, keepdims=True))
    o_ref[...] = (_rms_rows(o, gsub_ref[...]) * (1.0 - lam_init)).astype(BF16)


SAMPLE_PARTS = 2


def _sample_pages(part, lam, k_refs, v_refs, q_ref, kn_ref, vn_ref, bm_ref, gsub_ref, o_ref,
                  m_scr, l_scr, acc_scr, *, lam_init):
    n_rows = 2 * N_HEADS
    n_pages = len(k_refs)

    @pl.when(part == 0)
    def _():
        m_scr[...] = jnp.full(m_scr.shape, NEG_BIG, F32)
        l_scr[...] = jnp.zeros(l_scr.shape, F32)
        acc_scr[...] = jnp.zeros(acc_scr.shape, F32)

    q8 = q_ref[...]
    lane = lax.broadcasted_iota(jnp.int32, (N_HEADS, V_DIM), 1)
    qm = jnp.concatenate([jnp.where(lane < QK_HALF, q8, 0.0), jnp.where(lane < QK_HALF, 0.0, q8)], axis=0)
    qm_bf = qm.astype(BF16)

    s_parts = [_dot_nt(qm_bf, k_refs[p][...].reshape(PAGE_ROWS, V_DIM).astype(BF16)) for p in range(n_pages)]
    s = jnp.concatenate(s_parts, axis=-1) + bm_ref[...]
    kn = kn_ref[...].astype(BF16).astype(F32)
    s_new = jnp.sum(qm_bf.astype(F32) * jnp.concatenate([kn, kn], axis=0), axis=-1, keepdims=True)
    s_new = jnp.where(part == SAMPLE_PARTS - 1, s_new, NEG_BIG)

    m_prev = m_scr[...]
    m = jnp.maximum(jnp.maximum(jnp.max(s, axis=-1, keepdims=True), s_new), m_prev)
    p = jnp.exp(s - m)
    p_new = jnp.exp(s_new - m)
    keep = jnp.exp(m_prev - m)
    l = keep * l_scr[...] + jnp.sum(p, axis=-1, keepdims=True) + p_new
    p_bf = p.astype(BF16)
    acc = keep * acc_scr[...]
    for pg in range(n_pages):
        acc = acc + _dot(p_bf[:, pg * PAGE_ROWS:(pg + 1) * PAGE_ROWS],
                         v_refs[pg][...].reshape(PAGE_ROWS, V_DIM).astype(BF16))
    vn = vn_ref[...].astype(BF16).astype(F32)
    acc = acc + p_new.astype(BF16).astype(F32) * jnp.concatenate([vn, vn], axis=0)
    m_scr[...] = m
    l_scr[...] = l
    acc_scr[...] = acc

    w = acc / l
    o = w[0:N_HEADS] - lam * w[N_HEADS:n_rows]
    o_ref[...] = _rms_rows(o, gsub_ref[...]) * (1.0 - lam_init)


def _attention_kernel(pt_ref, q2_ref, k2_ref, v_ref, *refs, tq, n_page_refs, lam_init):
    del pt_ref
    k_refs = refs[:n_page_refs]
    v_refs = refs[n_page_refs:2 * n_page_refs]
    (qs_ref, kn_ref, vn_ref, bm_ref, lq1_ref, lk1_ref, lq2_ref, lk2_ref, gsub_ref,
     o_ref, os_ref, s_scr, v2_scr, acc_scr, sm_scr, sl_scr, sacc_scr) = refs[2 * n_page_refs:]
    i = pl.program_id(2)
    lam = _diff_lambda(lq1_ref[...], lk1_ref[...], lq2_ref[...], lk2_ref[...], lam_init)
    first_step = (pl.program_id(0) == 0) & (pl.program_id(1) == 0) & (i == 0)
    _prompt_query_block(i, first_step, lam, q2_ref, k2_ref, v_ref, gsub_ref, o_ref, s_scr, v2_scr, acc_scr,
                        tq=tq, lam_init=lam_init)
    _sample_pages(i % SAMPLE_PARTS, lam, k_refs, v_refs, qs_ref, kn_ref, vn_ref, bm_ref, gsub_ref, os_ref,
                  sm_scr, sl_scr, sacc_scr, lam_init=lam_init)


def _attention(page_table, q2, k2, vh, ck, cv, q_s, kn_s, vn_s, bm, lq1, lk1, lq2, lk2, gsub, *,
               layer, lam_init, tq=512):
    batch, _, _, seq, _ = q2.shape
    nq = seq // tq
    nb, n_pages = page_table.shape
    assert nq % SAMPLE_PARTS == 0 and batch * N_HEADS * nq == nb * SAMPLE_PARTS
    assert n_pages % SAMPLE_PARTS == 0
    pps = n_pages // SAMPLE_PARTS
    seqs_per_head = nq // SAMPLE_PARTS
    pt_flat = page_table.reshape(nb * n_pages)

    def sample_seq(b, h, i):
        return (b * N_HEADS + h) * seqs_per_head + i // SAMPLE_PARTS

    def page_spec(p):
        def idx(b, h, i, pt):
            return (layer, pt[sample_seq(b, h, i) * n_pages + (i % SAMPLE_PARTS) * pps + p], 0, 0, 0)
        return pl.BlockSpec((None, None, PAGE, N_HEADS, V_DIM), idx)

    srow = pl.BlockSpec((None, N_HEADS, V_DIM), lambda b, h, i, pt: (sample_seq(b, h, i), 0, 0))
    vec = lambda n: pl.BlockSpec((1, n), lambda b, h, i, pt: (0, 0))
    in_specs = (
        [pl.BlockSpec((None, None, 2, tq, V_DIM), lambda b, h, i, pt: (b, h, 0, i, 0)),
         pl.BlockSpec((None, None, 2, seq, V_DIM), lambda b, h, i, pt: (b, h, 0, 0, 0)),
         pl.BlockSpec((None, None, seq, V_DIM), lambda b, h, i, pt: (b, h, 0, 0))]
        + [page_spec(p) for p in range(pps)] + [page_spec(p) for p in range(pps)]
        + [srow, srow, srow,
           pl.BlockSpec((None, 2 * N_HEADS, pps * PAGE_ROWS), lambda b, h, i, pt: (i % SAMPLE_PARTS, 0, 0)),
           vec(QK_HALF), vec(QK_HALF), vec(QK_HALF), vec(QK_HALF), vec(V_DIM)])
    grid_spec = pltpu.PrefetchScalarGridSpec(
        num_scalar_prefetch=1, grid=(batch, N_HEADS, nq), in_specs=in_specs,
        out_specs=(pl.BlockSpec((None, tq, V_DIM), lambda b, h, i, pt: (b, i, h)), srow),
        scratch_shapes=[pltpu.VMEM((2, nq, tq, tq), F32), pltpu.VMEM((nq, 2 * tq, 2 * V_DIM), BF16),
                        pltpu.VMEM((tq, 2 * V_DIM), F32),
                        pltpu.VMEM((2 * N_HEADS, 1), F32), pltpu.VMEM((2 * N_HEADS, 1), F32),
                        pltpu.VMEM((2 * N_HEADS, V_DIM), F32)])
    return pl.pallas_call(
        functools.partial(_attention_kernel, tq=tq, n_page_refs=pps, lam_init=lam_init),
        out_shape=(jax.ShapeDtypeStruct((batch, seq, ATT_W), BF16),
                   jax.ShapeDtypeStruct((nb, N_HEADS, V_DIM), F32)),
        grid_spec=grid_spec, compiler_params=_cparams(3, 52), name="attention",
    )(pt_flat, q2, k2, vh, *([ck] * pps), *([cv] * pps), q_s, kn_s, vn_s, bm, lq1, lk1, lq2, lk2, gsub)


INFO_E1, INFO_E2, INFO_W1, INFO_W2, INFO_R1, INFO_R2 = 0, 1, 2, 3, 4, 5


def _lane_min_index(mask, lane):
    return jnp.min(jnp.where(mask, lane, float(LANES)), axis=-1, keepdims=True)


def _out_proj_kernel(o_ref, yc_ref, x_ref, w_ref, gffn_ref, wr2_ref, br_ref, tri_ref, cnt_in_ref,
                     xm_ref, h2_ref, info_ref, cnt_ref, cnt_scr, *, tm):
    i = pl.program_id(0)

    @pl.when(i == 0)
    def _():
        cnt_scr[...] = cnt_in_ref[...]

    xm = x_ref[...] + _dot(o_ref[...], w_ref[0:ATT_W, :]) + _dot(yc_ref[...], w_ref[ATT_W:D_MODEL, :])
    xm_ref[...] = xm
    h2 = _rms_rows(xm, gffn_ref[...])
    h2_ref[...] = h2

    hh = h2.astype(BF16)
    hl = (h2 - hh.astype(F32)).astype(BF16)
    parts = _dot(jnp.concatenate([hh, hl], axis=0), wr2_ref[...])
    logits = (parts[0:tm, 0:LANES] + parts[0:tm, LANES:2 * LANES]
              + parts[tm:2 * tm, 0:LANES] + parts[tm:2 * tm, LANES:2 * LANES]) + br_ref[...]

    lane = lax.broadcasted_iota(jnp.int32, (tm, LANES), 1).astype(F32)
    is_grp = lane < N_GROUPS
    lg = jnp.where(is_grp, logits, NEG_BIG)
    mg = jnp.max(lg, axis=-1, keepdims=True)
    g_star = _lane_min_index(is_grp & (lg == mg), lane)
    p_group = 1.0 / jnp.sum(jnp.where(is_grp, jnp.exp(lg - mg), 0.0), axis=-1, keepdims=True)

    e_lo = N_GROUPS + g_star * EPG
    in_grp = (lane >= e_lo) & (lane < e_lo + EPG)
    le = jnp.where(in_grp, logits, NEG_BIG)
    m1 = jnp.max(le, axis=-1, keepdims=True)
    ex = jnp.where(in_grp, jnp.exp(le - m1), 0.0)
    pe = ex / jnp.sum(ex, axis=-1, keepdims=True)
    pe_m = jnp.where(in_grp, pe, -1.0)
    p1 = jnp.max(pe_m, axis=-1, keepdims=True)
    i1 = _lane_min_index(pe_m == p1, lane)
    pe_m2 = jnp.where(lane == i1, -1.0, pe_m)
    p2 = jnp.max(pe_m2, axis=-1, keepdims=True)
    i2 = _lane_min_index(pe_m2 == p2, lane)
    denom = p1 + p2
    w1 = p_group * p1 / denom
    w2 = p_group * p2 / denom
    e1 = i1 - N_GROUPS
    e2 = i2 - N_GROUPS

    oh1 = lane == e1
    oh2 = lane == e2
    oh = jnp.where(oh1 | oh2, 1.0, 0.0)
    prefix = _dot(tri_ref[...], oh.astype(BF16)) + cnt_scr[...]
    r1 = jnp.sum(jnp.where(oh1, prefix, 0.0), axis=-1, keepdims=True)
    r2 = jnp.sum(jnp.where(oh2, prefix, 0.0), axis=-1, keepdims=True)
    cnt_scr[...] = cnt_scr[...] + jnp.sum(oh, axis=0, keepdims=True)
    cnt_ref[...] = cnt_scr[...]

    info = jnp.zeros((tm, LANES), F32)
    for idx, val in ((INFO_E1, e1), (INFO_E2, e2), (INFO_W1, w1), (INFO_W2, w2),
                     (INFO_R1, r1), (INFO_R2, r2)):
        info = jnp.where(lane == idx, val, info)
    info_ref[...] = info


def _out_proj(o, yc, x, w_bf, gffn, wr2, br, tri, cnt_in, *, tm):
    t = x.shape[0]
    row = lambda i: (i, 0)
    in_specs = [
        pl.BlockSpec((tm, ATT_W), row), pl.BlockSpec((tm, CONV_W), row), pl.BlockSpec((tm, D_MODEL), row),
        _const_spec((D_MODEL, D_MODEL), single_buffer=True), _const_spec((1, D_MODEL)),
        _const_spec((D_MODEL, 2 * LANES)), _const_spec((1, LANES)),
        _const_spec((tm, tm)), _const_spec((1, LANES)),
    ]
    out_specs = (
        pl.BlockSpec((tm, D_MODEL), row),
        pl.BlockSpec((tm, D_MODEL), row),
        pl.BlockSpec((tm, LANES), row),
        _const_spec((1, LANES)),
    )
    out_shape = (
        jax.ShapeDtypeStruct((t, D_MODEL), F32), jax.ShapeDtypeStruct((t, D_MODEL), F32),
        jax.ShapeDtypeStruct((t, LANES), F32), jax.ShapeDtypeStruct((1, LANES), F32),
    )
    return pl.pallas_call(
        functools.partial(_out_proj_kernel, tm=tm),
        out_shape=out_shape, grid=(t // tm,), in_specs=in_specs, out_specs=out_specs,
        scratch_shapes=[pltpu.VMEM((1, LANES), F32)],
        compiler_params=_cparams(1, 56), name="out_proj",
    )(o, yc, x, w_bf, gffn, wr2, br, tri, cnt_in)


def _row_copy(src, dst, sem):
    return pltpu.make_async_copy(src, dst, sem)


ISSUE_UNROLL = 8


def _for_each_row(n, fn):
    def chunk(k, c):
        for u in range(ISSUE_UNROLL):
            fn(k * ISSUE_UNROLL + u)
        return c
    lax.fori_loop(0, n // ISSUE_UNROLL, chunk, 0)


def _rows(ref, first, n=1):
    return ref.at[pl.ds(first, n)]


def _dispatch_kernel(d1_ref, d2_ref, s1_ref, s2_ref, fill_ref, end_ref, na_ref,
                     h2p_ref, h2s_ref, xb_hbm, zero_scr, sem, zsem, *, tm, n_sample, n_blocks):
    step = pl.program_id(0)

    @pl.when(step == 0)
    def _():
        zero_scr[...] = jnp.zeros(zero_scr.shape, F32)

        def pad_rows(fn):
            def per_expert(e, c):
                return lax.fori_loop(fill_ref[e], end_ref[e], lambda r, c2: fn(r, c2), c)
            lax.fori_loop(0, N_EXPERTS, per_expert, 0)

        def tail_blocks(fn):
            lax.fori_loop(na_ref[0], n_blocks, lambda b, c: fn(b, c), 0)

        def start_row(r, c):
            _row_copy(_rows(zero_scr, 0), _rows(xb_hbm, r), zsem).start()
            return c

        def wait_row(r, c):
            _row_copy(_rows(zero_scr, 0), _rows(xb_hbm, 0), zsem).wait()
            return c

        def start_blk(b, c):
            _row_copy(zero_scr, _rows(xb_hbm, b * MOE_BLK, MOE_BLK), zsem).start()
            return c

        def wait_blk(b, c):
            _row_copy(zero_scr, _rows(xb_hbm, 0, MOE_BLK), zsem).wait()
            return c

        pad_rows(start_row)
        tail_blocks(start_blk)

        def sample_row(r, c):
            _row_copy(_rows(h2s_ref, r), _rows(xb_hbm, s1_ref[r]), zsem).start()
            _row_copy(_rows(h2s_ref, r), _rows(xb_hbm, s2_ref[r]), zsem).start()
            return c

        lax.fori_loop(0, n_sample, sample_row, 0)
        pad_rows(wait_row)
        tail_blocks(wait_blk)
        _row_copy(h2s_ref, _rows(xb_hbm, 0, n_sample), zsem).wait()
        _row_copy(h2s_ref, _rows(xb_hbm, 0, n_sample), zsem).wait()

    base = step * tm

    def issue(r):
        t = base + r
        _row_copy(_rows(h2p_ref, r), _rows(xb_hbm, d1_ref[t]), sem).start(priority=0)
        _row_copy(_rows(h2p_ref, r), _rows(xb_hbm, d2_ref[t]), sem).start(priority=1)

    _for_each_row(tm, issue)
    _row_copy(h2p_ref, _rows(xb_hbm, 0, tm), sem).wait()
    _row_copy(h2p_ref, _rows(xb_hbm, 0, tm), sem).wait()


def _dispatch(d1_p, d2_p, d1_s, d2_s, fill, end, n_active, h2_p, h2_s, *, n_blocks, tm):
    n_p = h2_p.shape[0]
    n_s = h2_s.shape[0]
    grid_spec = pltpu.PrefetchScalarGridSpec(
        num_scalar_prefetch=7, grid=(n_p // tm,),
        in_specs=[pl.BlockSpec((tm, D_MODEL), lambda i, *_: (i, 0)),
                  pl.BlockSpec((n_s, D_MODEL), lambda i, *_: (0, 0))],
        out_specs=pl.BlockSpec(memory_space=pl.ANY),
        scratch_shapes=[pltpu.VMEM((MOE_BLK, D_MODEL), F32),
                        pltpu.SemaphoreType.DMA(()), pltpu.SemaphoreType.DMA(())])
    return pl.pallas_call(
        functools.partial(_dispatch_kernel, tm=tm, n_sample=n_s, n_blocks=n_blocks),
        out_shape=jax.ShapeDtypeStruct((n_blocks * MOE_BLK, D_MODEL), F32), grid_spec=grid_spec,
        compiler_params=_cparams(1, 32), name="moe_dispatch",
    )(d1_p, d2_p, d1_s, d2_s, fill, end, n_active, h2_p, h2_s)


def _experts_kernel(be_ref, na_ref, nxt_ref, xb_ref, wg_hbm, wu_hbm, wd_hbm, yb_ref,
                    wg_f32, wu_f32, wd_f32, wg_scr, wu_scr, wd_scr, slot_ref, sems, *, layer):
    b = pl.program_id(0)
    active = b < na_ref[0]
    expert = be_ref[b]
    new_expert = (b == 0) | (expert != be_ref[jnp.maximum(b - 1, 0)])

    def weight_copies(e, slot):
        return [pltpu.make_async_copy(src.at[layer, e], dst.at[slot], sems.at[slot])
                for src, dst in ((wg_hbm, wg_f32), (wu_hbm, wu_f32), (wd_hbm, wd_f32))]

    @pl.when(b == 0)
    def _():
        slot_ref[0] = 0
        for cp in weight_copies(expert, 0):
            cp.start()

    @pl.when(active & new_expert)
    def _():
        slot = slot_ref[0]
        for cp in weight_copies(expert, slot):
            cp.wait()
        wg_scr[...] = wg_f32[slot].astype(BF16)
        wu_scr[...] = wu_f32[slot].astype(BF16)
        wd_scr[...] = wd_f32[slot].astype(BF16)
        nxt = nxt_ref[expert]

        @pl.when(nxt >= 0)
        def _():
            for cp in weight_copies(nxt, 1 - slot):
                cp.start()

        slot_ref[0] = 1 - slot

    @pl.when(active)
    def _():
        x = xb_ref[...].astype(BF16)
        g = _dot(x, wg_scr[...])
        u = _dot(x, wu_scr[...])
        a = (g * (1.0 / (1.0 + jnp.exp(-g))) * u).astype(BF16)
        yb_ref[...] = _dot(a, wd_scr[...])

    @pl.when(pl.program_id(0) >= na_ref[0])
    def _():
        yb_ref[...] = jnp.zeros(yb_ref.shape, F32)


def _experts(block_e, n_active, next_expert, xb, wg, wu, wd, *, layer):
    n_blocks = xb.shape[0] // MOE_BLK
    blk = lambda b, be, na, nx: (jnp.minimum(b, na[0] - 1), 0)
    grid_spec = pltpu.PrefetchScalarGridSpec(
        num_scalar_prefetch=3, grid=(n_blocks,),
        in_specs=[
            pl.BlockSpec((MOE_BLK, D_MODEL), blk),
            pl.BlockSpec(memory_space=pl.ANY), pl.BlockSpec(memory_space=pl.ANY), pl.BlockSpec(memory_space=pl.ANY),
        ],
        out_specs=pl.BlockSpec((MOE_BLK, D_MODEL), lambda b, be, na, nx: (b, 0)),
        scratch_shapes=[pltpu.VMEM((2, D_MODEL, D_EXPERT), F32), pltpu.VMEM((2, D_MODEL, D_EXPERT), F32),
                        pltpu.VMEM((2, D_EXPERT, D_MODEL), F32),
                        pltpu.VMEM((D_MODEL, D_EXPERT), BF16), pltpu.VMEM((D_MODEL, D_EXPERT), BF16),
                        pltpu.VMEM((D_EXPERT, D_MODEL), BF16),
                        pltpu.SMEM((1,), jnp.int32), pltpu.SemaphoreType.DMA((2,))])
    return pl.pallas_call(
        functools.partial(_experts_kernel, layer=layer),
        out_shape=jax.ShapeDtypeStruct(xb.shape, F32), grid_spec=grid_spec,
        compiler_params=_cparams(1, 52), name="moe_experts",
    )(block_e, n_active, next_expert, xb, wg, wu, wd)


def _combine_kernel(d1_ref, d2_ref, xm_ref, info_ref, yb_hbm, out_ref, a_scr, b_scr, sems, *, tm, n_tiles):
    i = pl.program_id(0)

    def fetch(tile, slot):
        base = tile * tm

        def issue(r):
            t = base + r
            _row_copy(_rows(yb_hbm, d1_ref[t]), _rows(a_scr.at[slot], r), sems.at[slot]).start(priority=0)
            _row_copy(_rows(yb_hbm, d2_ref[t]), _rows(b_scr.at[slot], r), sems.at[slot]).start(priority=1)

        _for_each_row(tm, issue)

    @pl.when(i == 0)
    def _():
        fetch(0, 0)

    @pl.when(i + 1 < n_tiles)
    def _():
        fetch(i + 1, (i + 1) % 2)

    slot = i % 2
    _row_copy(_rows(yb_hbm, 0, tm), a_scr.at[slot], sems.at[slot]).wait()
    _row_copy(_rows(yb_hbm, 0, tm), b_scr.at[slot], sems.at[slot]).wait()

    info = info_ref[...]
    w1 = info[:, INFO_W1:INFO_W1 + 1]
    w2 = info[:, INFO_W2:INFO_W2 + 1]
    out_ref[...] = xm_ref[...] + (a_scr[slot] * w1 + b_scr[slot] * w2)


def _combine(dest1, dest2, xm_buf, info_buf, yb, *, tm):
    n_rows = xm_buf.shape[0]
    n_tiles = n_rows // tm
    grid_spec = pltpu.PrefetchScalarGridSpec(
        num_scalar_prefetch=2, grid=(n_tiles,),
        in_specs=[
            pl.BlockSpec((tm, D_MODEL), lambda i, a, b: (i, 0)),
            pl.BlockSpec((tm, LANES), lambda i, a, b: (i, 0)),
            pl.BlockSpec(memory_space=pl.ANY),
        ],
        out_specs=pl.BlockSpec((tm, D_MODEL), lambda i, a, b: (i, 0)),
        scratch_shapes=[pltpu.VMEM((2, tm, D_MODEL), F32), pltpu.VMEM((2, tm, D_MODEL), F32),
                        pltpu.SemaphoreType.DMA((2,))])
    return pl.pallas_call(
        functools.partial(_combine_kernel, tm=tm, n_tiles=n_tiles),
        out_shape=jax.ShapeDtypeStruct((n_rows, D_MODEL), F32), grid_spec=grid_spec,
        compiler_params=_cparams(1, 32), name="moe_combine",
    )(dest1, dest2, xm_buf, info_buf, yb)


def _layer(l, xp, xs, cache_k, cache_v, state_conv, page_table, norm_mix, w_in, g_q, g_k,
           lam_q1, lam_k1, lam_q2, lam_k2, g_sub, conv_w, w_out, norm_ffn, w_rg, b_rg, w_re, b_re,
           w_gate, w_up, w_down):
    batch, seq, _ = xp.shape
    nb = xs.shape[0]
    n_pages = page_table.shape[1]
    past = n_pages * PAGE
    t_p = batch * seq
    n_tok = t_p + nb
    lam_init = 0.8 - 0.6 * math.exp(-0.3 * l)

    w_in_bf = w_in[l].astype(BF16)
    w_out_bf = w_out[l].astype(BF16)
    gmix = norm_mix[l].reshape(1, D_MODEL)
    gffn = norm_ffn[l].reshape(1, D_MODEL)
    gq = jnp.tile(g_q[l], ATT_W // QK_HALF).reshape(1, ATT_W)
    gk = jnp.tile(g_k[l], ATT_W // QK_HALF).reshape(1, ATT_W)
    gsub = g_sub[l].reshape(1, V_DIM)
    grp = jnp.arange(IN_CHUNK) // QK_HALF
    gsum = jnp.where(grp[:, None] == grp[None, :], 1.0 / QK_HALF, 0.0).astype(BF16)
    lq1, lk1, lq2, lk2 = (v[l].reshape(1, QK_HALF) for v in (lam_q1, lam_k1, lam_q2, lam_k2))
    slopes = 2.0 ** (-8.0 * jnp.arange(1, N_HEADS + 1, dtype=F32) / N_HEADS)
    lane = jnp.arange(LANES)
    sl = slopes * LOG2E
    s_hi = sl.astype(BF16).astype(F32)
    s_mid = (sl - s_hi).astype(BF16).astype(F32)
    s_lo = (sl - s_hi - s_mid).astype(BF16).astype(F32)
    terms = jnp.stack([s_hi, s_hi, s_mid, s_mid, s_lo, s_lo], axis=1)

    def query_alibi(first_lane):
        rel = lane - first_lane
        return jnp.where((rel >= 0) & (rel < ALIBI_LANES), terms[:, jnp.clip(rel, 0, ALIBI_LANES - 1)], 0.0)

    qal = jnp.stack([query_alibi(QK_HALF), query_alibi(0)], axis=1)
    w_r = jnp.zeros((D_MODEL, LANES), F32)
    w_r = w_r.at[:, 0:N_GROUPS].set(w_rg[l]).at[:, N_GROUPS:N_GROUPS + N_EXPERTS].set(w_re[l])
    wrh = w_r.astype(BF16)
    wr2 = jnp.concatenate([wrh, (w_r - wrh.astype(F32)).astype(BF16)], axis=1)
    br = jnp.zeros((1, LANES), F32)
    br = br.at[0, 0:N_GROUPS].set(b_rg[l]).at[0, N_GROUPS:N_GROUPS + N_EXPERTS].set(b_re[l])

    x2p = xp.reshape(t_p, D_MODEL)
    x2s = xs.reshape(nb, D_MODEL)
    q2, k2, vh, kf_p, vf_p, yc_p, cs_p = _in_proj_prompt(
        x2p, gmix, w_in_bf, gq, gk, gsum, conv_w[l], qal, batch=batch, seq=seq)

    prev0 = state_conv[l, :, 0, :]
    prev1 = state_conv[l, :, 1, :]
    q_s, kf_s, vf_s, yc_s, u_s = _in_proj_sample(x2s, gmix, w_in_bf, gq, gk, gsum, conv_w[l], prev0, prev1)
    row_pos = jnp.arange(past * N_HEADS, dtype=jnp.int32) // N_HEADS
    row_head = jnp.arange(past * N_HEADS, dtype=jnp.int32) % N_HEADS
    q_head = jnp.arange(2 * N_HEADS, dtype=jnp.int32) % N_HEADS
    bias = -(slopes[q_head][:, None] * (past - row_pos).astype(F32)[None, :])
    bm = jnp.where(q_head[:, None] == row_head[None, :], bias, NEG_BIG)
    bm = jnp.moveaxis(bm.reshape(2 * N_HEADS, SAMPLE_PARTS, past * N_HEADS // SAMPLE_PARTS), 1, 0)
    o_p, o_s = _attention(page_table, q2, k2, vh, cache_k, cache_v, q_s, kf_s, vf_s, bm, lq1, lk1, lq2, lk2, gsub,
                          layer=l, lam_init=lam_init)
    o_s = o_s.reshape(nb, ATT_W).astype(BF16)

    cnt0 = jnp.zeros((1, LANES), F32)
    tm_p, tm_s = 512, nb
    tri = lambda n: (jnp.arange(n)[:, None] > jnp.arange(n)[None, :]).astype(BF16)
    xm_p, h2_p, info_p, cnt1 = _out_proj(
        o_p.reshape(t_p, ATT_W), yc_p, x2p, w_out_bf, gffn, wr2, br, tri(tm_p), cnt0, tm=tm_p)
    xm_s, h2_s, info_s, cnt2 = _out_proj(
        o_s, yc_s, x2s, w_out_bf, gffn, wr2, br, tri(tm_s), cnt1, tm=tm_s)

    counts = cnt2[0, 0:N_EXPERTS].astype(jnp.int32)
    padded = (counts + MOE_BLK - 1) // MOE_BLK * MOE_BLK
    pad_end = jnp.cumsum(padded)
    pad_start = pad_end - padded
    n_blocks = -(-2 * n_tok // MOE_BLK) + N_EXPERTS
    blk_row = jnp.arange(n_blocks, dtype=jnp.int32) * MOE_BLK
    block_e = jnp.minimum(jnp.sum((pad_end[None, :] <= blk_row[:, None]).astype(jnp.int32), axis=1),
                          N_EXPERTS - 1)
    n_active = (pad_end[-1:] // MOE_BLK).astype(jnp.int32)
    experts = jnp.arange(N_EXPERTS, dtype=jnp.int32)

    def slots(info):
        def one(e_col, r_col):
            e = info[:, e_col].astype(jnp.int32)
            start = jnp.sum(jnp.where(e[:, None] == experts[None, :], pad_start[None, :], 0), axis=1)
            return start + info[:, r_col].astype(jnp.int32)
        return one(INFO_E1, INFO_R1), one(INFO_E2, INFO_R2)

    d1_p, d2_p = slots(info_p)
    d1_s, d2_s = slots(info_s)

    xb = _dispatch(d1_p, d2_p, d1_s, d2_s, pad_start + counts, pad_end, n_active, h2_p, h2_s,
                   n_blocks=n_blocks, tm=512)
    later = (experts[None, :] > experts[:, None]) & (counts[None, :] > 0)
    nxt = jnp.min(jnp.where(later, experts[None, :], N_EXPERTS), axis=1)
    next_expert = jnp.where(nxt < N_EXPERTS, nxt, -1).astype(jnp.int32)
    yb = _experts(block_e, n_active, next_expert, xb, w_gate, w_up, w_down, layer=l)
    y_p = _combine(d1_p, d2_p, xm_p, info_p, yb, tm=256)
    y_s = _combine(d1_s, d2_s, xm_s, info_s, yb, tm=128)

    conv_s = jnp.stack([prev1, u_s], axis=1)
    return (y_p.reshape(batch, seq, D_MODEL), y_s.reshape(nb, 1, D_MODEL),
            kf_p.reshape(batch, seq, N_HEADS, V_DIM), vf_p.reshape(batch, seq, N_HEADS, V_DIM), cs_p,
            kf_s.reshape(nb, 1, N_HEADS, V_DIM), vf_s.reshape(nb, 1, N_HEADS, V_DIM), conv_s)


def kernel(x_prompt, x_sample, cache_k, cache_v, state_conv, page_table, norm_mix, w_in, g_q, g_k,
           lam_q1, lam_k1, lam_q2, lam_k2, g_sub, conv_w, w_out, norm_ffn, w_router_group, b_router_group,
           w_router_expert, b_router_expert, w_gate, w_up, w_down):
    depth = w_in.shape[0]
    xp, xs = x_prompt, x_sample
    outs = [[] for _ in range(6)]
    for l in range(depth):
        xp, xs, *rest = _layer(
            l, xp, xs, cache_k, cache_v, state_conv, page_table, norm_mix, w_in, g_q, g_k,
            lam_q1, lam_k1, lam_q2, lam_k2, g_sub, conv_w, w_out, norm_ffn, w_router_group, b_router_group,
            w_router_expert, b_router_expert, w_gate, w_up, w_down)
        for acc, val in zip(outs, rest):
            acc.append(val)
    return (xp, xs) + tuple(jnp.stack(v, axis=0) for v in outs)
```

```python
import functools
import math

import jax
import jax.numpy as jnp
from jax import lax
from jax.experimental import pallas as pl
from jax.experimental.pallas import tpu as pltpu

D_MODEL = 2048
N_HEADS = 8
V_DIM = 128
QK_HALF = 64
ATT_W = N_HEADS * V_DIM
CONV_W = D_MODEL - ATT_W
CONV_K = 3
IN_COLS = 3 * ATT_W + 3 * CONV_W
N_GROUPS = 4
EPG = 8
N_EXPERTS = N_GROUPS * EPG
D_EXPERT = D_MODEL // 4
PAGE = 128
EPS = 1e-6

LANES = 128
SUBLANES = 8
MOE_BLK = 256
NEG_BIG = -1e30
POS_SPLIT = 16
ALIBI_LANES = 6
LOG2E = math.log2(math.e)
PAGE_ROWS = PAGE * N_HEADS

F32 = jnp.float32
BF16 = jnp.bfloat16


def _cparams(n_axes, vmem_mb):
    return pltpu.CompilerParams(
        dimension_semantics=("arbitrary",) * n_axes,
        vmem_limit_bytes=vmem_mb * 1024 * 1024)


def _const_spec(shape, single_buffer=False):
    nd = len(shape)
    idx = lambda *_: (0,) * nd
    if single_buffer:
        return pl.BlockSpec(shape, idx, pipeline_mode=pl.Buffered(1))
    return pl.BlockSpec(shape, idx)


def _dot(a, b):
    return jnp.dot(a, b, preferred_element_type=F32)


def _dot_nt(a, b):
    return lax.dot_general(a, b, (((1,), (1,)), ((), ())), preferred_element_type=F32)


def _rms_rows(x, gain):
    ms = jnp.mean(x * x, axis=-1, keepdims=True)
    return x * lax.rsqrt(ms + EPS) * gain


def _diff_lambda(lq1, lk1, lq2, lk2, lam_init):
    s1 = jnp.sum(lq1 * lk1, axis=-1, keepdims=True)
    s2 = jnp.sum(lq2 * lk2, axis=-1, keepdims=True)
    return jnp.exp(s1) - jnp.exp(s2) + lam_init


IN_CHUNK = 512


def _half_norm(p, gsum, gain):
    ms = _dot((p * p).astype(BF16), gsum)
    return p * lax.rsqrt(ms + EPS) * gain


def _in_proj_qkv_kernel(x_ref, gmix_ref, w_ref, gq_ref, gk_ref, gsum_ref, qal_ref,
                        q2_ref, k2_ref, vh_ref, kf_ref, vf_ref, h_scr, *, tm, tiles_per_seq):
    i = pl.program_id(0)
    h_scr[...] = _rms_rows(x_ref[...], gmix_ref[...]).astype(BF16)
    gsum = gsum_ref[...]
    lane = lax.broadcasted_iota(jnp.int32, (tm, LANES), 1)
    first_half = lane < QK_HALF
    heads_per_chunk = IN_CHUNK // V_DIM

    pos = lax.broadcasted_iota(jnp.int32, (tm, LANES), 0) + (i % tiles_per_seq) * tm
    pos_hi = (pos // POS_SPLIT * POS_SPLIT).astype(F32)
    pos_lo = (pos % POS_SPLIT).astype(F32)

    def key_alibi(first_lane):
        rel = lane - first_lane
        return jnp.where((rel >= 0) & (rel < ALIBI_LANES), jnp.where(rel % 2 == 0, pos_hi, pos_lo), 0.0)

    kal_a = key_alibi(QK_HALF)
    kal_b = key_alibi(0)

    for c in range(ATT_W // IN_CHUNK):
        cols = slice(c * IN_CHUNK, (c + 1) * IN_CHUNK)
        p = _dot(h_scr[...], w_ref[:, c * IN_CHUNK:(c + 1) * IN_CHUNK])
        qn = _half_norm(p, gsum, gq_ref[:, cols]) * (LOG2E * QK_HALF ** -0.5)
        for hh in range(heads_per_chunk):
            qh = qn[:, hh * V_DIM:(hh + 1) * V_DIM]
            head = c * heads_per_chunk + hh
            q2_ref[head, 0] = jnp.where(first_half, qh, qal_ref[head, 0:1, :]).astype(BF16)
            q2_ref[head, 1] = jnp.where(first_half, qal_ref[head, 1:2, :], qh).astype(BF16)
        p = _dot(h_scr[...], w_ref[:, ATT_W + c * IN_CHUNK:ATT_W + (c + 1) * IN_CHUNK])
        kn = _half_norm(p, gsum, gk_ref[:, cols])
        kf_ref[:, cols] = kn
        for hh in range(heads_per_chunk):
            kh = kn[:, hh * V_DIM:(hh + 1) * V_DIM]
            head = c * heads_per_chunk + hh
            k2_ref[head, 0] = jnp.where(first_half, kh, kal_a).astype(BF16)
            k2_ref[head, 1] = jnp.where(first_half, kal_b, kh).astype(BF16)
        p = _dot(h_scr[...], w_ref[:, 2 * ATT_W + c * IN_CHUNK:2 * ATT_W + (c + 1) * IN_CHUNK])
        vf_ref[:, cols] = p
        for hh in range(heads_per_chunk):
            vh_ref[c * heads_per_chunk + hh] = p[:, hh * V_DIM:(hh + 1) * V_DIM].astype(BF16)


def _in_proj_conv_kernel(x_ref, gmix_ref, w_ref, cw_ref, yc_ref, cs_ref, h_scr, u_scr, *, tm, tiles_per_seq):
    i = pl.program_id(0)

    @pl.when(i % tiles_per_seq == 0)
    def _():
        u_scr[0:SUBLANES, :] = jnp.zeros((SUBLANES, CONV_W), F32)

    h_scr[...] = _rms_rows(x_ref[...], gmix_ref[...]).astype(BF16)
    for c in range(CONV_W // IN_CHUNK):
        cols = slice(c * IN_CHUNK, (c + 1) * IN_CHUNK)
        lo = c * IN_CHUNK
        bg = _dot(h_scr[...], w_ref[:, lo:lo + IN_CHUNK])
        cg = _dot(h_scr[...], w_ref[:, CONV_W + lo:CONV_W + lo + IN_CHUNK])
        xc = _dot(h_scr[...], w_ref[:, 2 * CONV_W + lo:2 * CONV_W + lo + IN_CHUNK])
        u = cg * xc
        u_scr[SUBLANES:SUBLANES + tm, cols] = u
        u1 = u_scr[SUBLANES - 1:SUBLANES - 1 + tm, cols]
        u2 = u_scr[SUBLANES - 2:SUBLANES - 2 + tm, cols]
        y = cw_ref[0:1, cols] * u2 + cw_ref[1:2, cols] * u1 + cw_ref[2:3, cols] * u
        yc_ref[:, cols] = (bg * y).astype(BF16)

    cs_ref[...] = u_scr[tm + SUBLANES - 2:tm + SUBLANES, :]
    u_scr[0:SUBLANES, :] = u_scr[tm:tm + SUBLANES, :]


def _in_proj_sample_kernel(x_ref, gmix_ref, w_ref, gq_ref, gk_ref, gsum_ref, cw_ref, p0_ref, p1_ref,
                           q_ref, kf_ref, vf_ref, yc_ref, u_ref, h_scr):
    h_scr[...] = _rms_rows(x_ref[...], gmix_ref[...]).astype(BF16)
    gsum = gsum_ref[...]
    heads_per_chunk = IN_CHUNK // V_DIM
    for c in range(ATT_W // IN_CHUNK):
        cols = slice(c * IN_CHUNK, (c + 1) * IN_CHUNK)
        p = _dot(h_scr[...], w_ref[:, c * IN_CHUNK:(c + 1) * IN_CHUNK])
        qn = _half_norm(p, gsum, gq_ref[:, cols]) * (QK_HALF ** -0.5)
        p = _dot(h_scr[...], w_ref[:, ATT_W + c * IN_CHUNK:ATT_W + (c + 1) * IN_CHUNK])
        kn = _half_norm(p, gsum, gk_ref[:, cols])
        vn = _dot(h_scr[...], w_ref[:, 2 * ATT_W + c * IN_CHUNK:2 * ATT_W + (c + 1) * IN_CHUNK])
        for hh in range(heads_per_chunk):
            head = c * heads_per_chunk + hh
            hcols = slice(hh * V_DIM, (hh + 1) * V_DIM)
            q_ref[:, head, :] = qn[:, hcols]
            kf_ref[:, head, :] = kn[:, hcols]
            vf_ref[:, head, :] = vn[:, hcols]
    base = 3 * ATT_W
    for c in range(CONV_W // IN_CHUNK):
        cols = slice(c * IN_CHUNK, (c + 1) * IN_CHUNK)
        lo = c * IN_CHUNK
        bg = _dot(h_scr[...], w_ref[:, base + lo:base + lo + IN_CHUNK])
        cg = _dot(h_scr[...], w_ref[:, base + CONV_W + lo:base + CONV_W + lo + IN_CHUNK])
        xc = _dot(h_scr[...], w_ref[:, base + 2 * CONV_W + lo:base + 2 * CONV_W + lo + IN_CHUNK])
        u = cg * xc
        u_ref[:, cols] = u
        y = cw_ref[0:1, cols] * p0_ref[:, cols] + cw_ref[1:2, cols] * p1_ref[:, cols] + cw_ref[2:3, cols] * u
        yc_ref[:, cols] = (bg * y).astype(BF16)


def _in_proj_prompt(x, gmix, w_bf, gq, gk, gsum, cw, qal, *, batch, seq, tm=512):
    t = batch * seq
    tps = seq // tm
    row = lambda i: (i, 0)
    half_cols = IN_COLS // 2
    x_spec = pl.BlockSpec((tm, D_MODEL), row)
    w_spec = lambda part: pl.BlockSpec((D_MODEL, half_cols), lambda i: (0, part), pipeline_mode=pl.Buffered(1))
    head_blk = lambda i: (i // tps, 0, 0, i % tps, 0)
    q2, k2, vh, kf, vf = pl.pallas_call(
        functools.partial(_in_proj_qkv_kernel, tm=tm, tiles_per_seq=tps),
        out_shape=(
            jax.ShapeDtypeStruct((batch, N_HEADS, 2, seq, V_DIM), BF16),
            jax.ShapeDtypeStruct((batch, N_HEADS, 2, seq, V_DIM), BF16),
            jax.ShapeDtypeStruct((batch, N_HEADS, seq, V_DIM), BF16),
            jax.ShapeDtypeStruct((t, ATT_W), F32),
            jax.ShapeDtypeStruct((t, ATT_W), F32),
        ),
        grid=(t // tm,),
        in_specs=[x_spec, _const_spec((1, D_MODEL)), w_spec(0), _const_spec((1, ATT_W)), _const_spec((1, ATT_W)),
                  _const_spec((IN_CHUNK, IN_CHUNK)), _const_spec((N_HEADS, 2, LANES))],
        out_specs=(
            pl.BlockSpec((None, N_HEADS, 2, tm, V_DIM), head_blk),
            pl.BlockSpec((None, N_HEADS, 2, tm, V_DIM), head_blk),
            pl.BlockSpec((None, N_HEADS, tm, V_DIM), lambda i: (i // tps, 0, i % tps, 0)),
            pl.BlockSpec((tm, ATT_W), row),
            pl.BlockSpec((tm, ATT_W), row),
        ),
        scratch_shapes=[pltpu.VMEM((tm, D_MODEL), BF16)],
        compiler_params=_cparams(1, 56), name="in_proj_qkv",
    )(x, gmix, w_bf, gq, gk, gsum, qal)
    yc, cs = pl.pallas_call(
        functools.partial(_in_proj_conv_kernel, tm=tm, tiles_per_seq=tps),
        out_shape=(
            jax.ShapeDtypeStruct((t, CONV_W), BF16),
            jax.ShapeDtypeStruct((batch, CONV_K - 1, CONV_W), F32),
        ),
        grid=(t // tm,),
        in_specs=[x_spec, _const_spec((1, D_MODEL)), w_spec(1), _const_spec((CONV_K, CONV_W))],
        out_specs=(
            pl.BlockSpec((tm, CONV_W), row),
            pl.BlockSpec((None, CONV_K - 1, CONV_W), lambda i: (i // tps, 0, 0)),
        ),
        scratch_shapes=[pltpu.VMEM((tm, D_MODEL), BF16), pltpu.VMEM((tm + SUBLANES, CONV_W), F32)],
        compiler_params=_cparams(1, 48), name="in_proj_conv",
    )(x, gmix, w_bf, cw)
    return q2, k2, vh, kf, vf, yc, cs


def _in_proj_sample(x, gmix, w_bf, gq, gk, gsum, cw, prev0, prev1):
    t = x.shape[0]
    full = lambda shape: _const_spec(shape)
    out_shape = (
        jax.ShapeDtypeStruct((t, N_HEADS, V_DIM), F32),
        jax.ShapeDtypeStruct((t, N_HEADS, V_DIM), F32),
        jax.ShapeDtypeStruct((t, N_HEADS, V_DIM), F32),
        jax.ShapeDtypeStruct((t, CONV_W), BF16),
        jax.ShapeDtypeStruct((t, CONV_W), F32),
    )
    in_specs = [
        full((t, D_MODEL)), full((1, D_MODEL)), _const_spec((D_MODEL, IN_COLS), single_buffer=True),
        full((1, ATT_W)), full((1, ATT_W)), full((IN_CHUNK, IN_CHUNK)), full((CONV_K, CONV_W)),
        full((t, CONV_W)), full((t, CONV_W)),
    ]
    return pl.pallas_call(
        _in_proj_sample_kernel, out_shape=out_shape, grid=(1,), in_specs=in_specs,
        out_specs=tuple(full(s.shape) for s in out_shape),
        scratch_shapes=[pltpu.VMEM((t, D_MODEL), BF16)],
        compiler_params=_cparams(1, 48), name="in_proj_sample",
    )(x, gmix, w_bf, gq, gk, gsum, cw, prev0, prev1)


def _prompt_query_block(i, first_step, lam, q2_ref, k2_ref, v_ref, gsub_ref, o_ref, s_scr, v2_scr, acc_scr, *,
                        tq, lam_init):
    n_lt = tq // LANES
    row = lax.broadcasted_iota(jnp.int32, (tq, tq), 0)
    col = lax.broadcasted_iota(jnp.int32, (tq, tq), 1)
    causal = col <= row

    def lane_tile_max(m_acc, s):
        for t in range(n_lt):
            m_acc = jnp.maximum(m_acc, s[:, t * LANES:(t + 1) * LANES])
        return m_acc

    def lane_tile_sum(l_acc, p):
        for t in range(n_lt):
            l_acc = l_acc + p[:, t * LANES:(t + 1) * LANES]
        return l_acc

    def scores(j, c):
        start = pl.multiple_of(j * tq, tq)
        return _dot_nt(q2_ref[c], k2_ref[c, pl.ds(start, tq), :])

    def pass1(j, carry):
        out = []
        for c in range(2):
            s = scores(j, c)
            s_scr[c, j] = s
            out.append(lane_tile_max(carry[c], s))
        return tuple(out)

    neg = jnp.full((tq, LANES), NEG_BIG, F32)
    m_acc = lax.fori_loop(0, i, pass1, (neg, neg))
    m_rows = []
    for c in range(2):
        s = jnp.where(causal, scores(i, c), NEG_BIG)
        s_scr[c, i] = s
        m_rows.append(jnp.max(lane_tile_max(m_acc[c], s), axis=-1, keepdims=True))

    @pl.when(first_step)
    def _():
        v2_scr[...] = jnp.zeros(v2_scr.shape, BF16)

    @pl.when(i == 0)
    def _():
        for j in range(v2_scr.shape[0]):
            v = v_ref[j * tq:(j + 1) * tq, :]
            v2_scr[j, 0:tq, 0:V_DIM] = v
            v2_scr[j, tq:2 * tq, V_DIM:2 * V_DIM] = v

    acc_scr[...] = jnp.zeros(acc_scr.shape, F32)

    def pass2(j, carry):
        l0, l1 = carry
        p0 = jnp.exp2(s_scr[0, j] - m_rows[0])
        p1 = jnp.exp2(s_scr[1, j] - m_rows[1])
        pcat = jnp.concatenate([p0.astype(BF16), p1.astype(BF16)], axis=1)
        acc_scr[...] += _dot(pcat, v2_scr[j])
        return lane_tile_sum(l0, p0), lane_tile_sum(l1, p1)

    z = jnp.zeros((tq, LANES), F32)
    l0, l1 = lax.fori_loop(0, i + 1, pass2, (z, z))
    a0 = acc_scr[:, 0:V_DIM]
    a1 = acc_scr[:, V_DIM:2 * V_DIM]
    o = a0 / jnp.sum(l0, axis=-1, keepdims=True) - lam * (a1 / jnp.sum(l1, axis=-1, keepdims=True))
    o_ref[...] = (_rms_rows(o, gsub_ref[...]) * (1.0 - lam_init)).astype(BF16)


SAMPLE_PARTS = 2


def _sample_pages(part, lam, k_refs, v_refs, q_ref, kn_ref, vn_ref, bm_ref, gsub_ref, o_ref,
                  m_scr, l_scr, acc_scr, *, lam_init):
    n_rows = 2 * N_HEADS
    n_pages = len(k_refs)

    @pl.when(part == 0)
    def _():
        m_scr[...] = jnp.full(m_scr.shape, NEG_BIG, F32)
        l_scr[...] = jnp.zeros(l_scr.shape, F32)
        acc_scr[...] = jnp.zeros(acc_scr.shape, F32)

    q8 = q_ref[...]
    lane = lax.broadcasted_iota(jnp.int32, (N_HEADS, V_DIM), 1)
    qm = jnp.concatenate([jnp.where(lane < QK_HALF, q8, 0.0), jnp.where(lane < QK_HALF, 0.0, q8)], axis=0)
    qm_bf = qm.astype(BF16)

    s_parts = [_dot_nt(qm_bf, k_refs[p][...].reshape(PAGE_ROWS, V_DIM).astype(BF16)) for p in range(n_pages)]
    s = jnp.concatenate(s_parts, axis=-1) + bm_ref[...]
    kn = kn_ref[...].astype(BF16).astype(F32)
    s_new = jnp.sum(qm_bf.astype(F32) * jnp.concatenate([kn, kn], axis=0), axis=-1, keepdims=True)
    s_new = jnp.where(part == SAMPLE_PARTS - 1, s_new, NEG_BIG)

    m_prev = m_scr[...]
    m = jnp.maximum(jnp.maximum(jnp.max(s, axis=-1, keepdims=True), s_new), m_prev)
    p = jnp.exp(s - m)
    p_new = jnp.exp(s_new - m)
    keep = jnp.exp(m_prev - m)
    l = keep * l_scr[...] + jnp.sum(p, axis=-1, keepdims=True) + p_new
    p_bf = p.astype(BF16)
    acc = keep * acc_scr[...]
    for pg in range(n_pages):
        acc = acc + _dot(p_bf[:, pg * PAGE_ROWS:(pg + 1) * PAGE_ROWS],
                         v_refs[pg][...].reshape(PAGE_ROWS, V_DIM).astype(BF16))
    vn = vn_ref[...].astype(BF16).astype(F32)
    acc = acc + p_new.astype(BF16).astype(F32) * jnp.concatenate([vn, vn], axis=0)
    m_scr[...] = m
    l_scr[...] = l
    acc_scr[...] = acc

    w = acc / l
    o = w[0:N_HEADS] - lam * w[N_HEADS:n_rows]
    o_ref[...] = _rms_rows(o, gsub_ref[...]) * (1.0 - lam_init)


def _attention_kernel(pt_ref, q2_ref, k2_ref, v_ref, *refs, tq, n_page_refs, lam_init):
    del pt_ref
    k_refs = refs[:n_page_refs]
    v_refs = refs[n_page_refs:2 * n_page_refs]
    (qs_ref, kn_ref, vn_ref, bm_ref, lq1_ref, lk1_ref, lq2_ref, lk2_ref, gsub_ref,
     o_ref, os_ref, s_scr, v2_scr, acc_scr, sm_scr, sl_scr, sacc_scr) = refs[2 * n_page_refs:]
    i = pl.program_id(2)
    lam = _diff_lambda(lq1_ref[...], lk1_ref[...], lq2_ref[...], lk2_ref[...], lam_init)
    first_step = (pl.program_id(0) == 0) & (pl.program_id(1) == 0) & (i == 0)
    _prompt_query_block(i, first_step, lam, q2_ref, k2_ref, v_ref, gsub_ref, o_ref, s_scr, v2_scr, acc_scr,
                        tq=tq, lam_init=lam_init)
    _sample_pages(i % SAMPLE_PARTS, lam, k_refs, v_refs, qs_ref, kn_ref, vn_ref, bm_ref, gsub_ref, os_ref,
                  sm_scr, sl_scr, sacc_scr, lam_init=lam_init)


def _attention(page_table, q2, k2, vh, ck, cv, q_s, kn_s, vn_s, bm, lq1, lk1, lq2, lk2, gsub, *,
               layer, lam_init, tq=512):
    batch, _, _, seq, _ = q2.shape
    nq = seq // tq
    nb, n_pages = page_table.shape
    assert nq % SAMPLE_PARTS == 0 and batch * N_HEADS * nq == nb * SAMPLE_PARTS
    assert n_pages % SAMPLE_PARTS == 0
    pps = n_pages // SAMPLE_PARTS
    seqs_per_head = nq // SAMPLE_PARTS
    pt_flat = page_table.reshape(nb * n_pages)

    def sample_seq(b, h, i):
        return (b * N_HEADS + h) * seqs_per_head + i // SAMPLE_PARTS

    def page_spec(p):
        def idx(b, h, i, pt):
            return (layer, pt[sample_seq(b, h, i) * n_pages + (i % SAMPLE_PARTS) * pps + p], 0, 0, 0)
        return pl.BlockSpec((None, None, PAGE, N_HEADS, V_DIM), idx)

    srow = pl.BlockSpec((None, N_HEADS, V_DIM), lambda b, h, i, pt: (sample_seq(b, h, i), 0, 0))
    vec = lambda n: pl.BlockSpec((1, n), lambda b, h, i, pt: (0, 0))
    in_specs = (
        [pl.BlockSpec((None, None, 2, tq, V_DIM), lambda b, h, i, pt: (b, h, 0, i, 0)),
         pl.BlockSpec((None, None, 2, seq, V_DIM), lambda b, h, i, pt: (b, h, 0, 0, 0)),
         pl.BlockSpec((None, None, seq, V_DIM), lambda b, h, i, pt: (b, h, 0, 0))]
        + [page_spec(p) for p in range(pps)] + [page_spec(p) for p in range(pps)]
        + [srow, srow, srow,
           pl.BlockSpec((None, 2 * N_HEADS, pps * PAGE_ROWS), lambda b, h, i, pt: (i % SAMPLE_PARTS, 0, 0)),
           vec(QK_HALF), vec(QK_HALF), vec(QK_HALF), vec(QK_HALF), vec(V_DIM)])
    grid_spec = pltpu.PrefetchScalarGridSpec(
        num_scalar_prefetch=1, grid=(batch, N_HEADS, nq), in_specs=in_specs,
        out_specs=(pl.BlockSpec((None, tq, V_DIM), lambda b, h, i, pt: (b, i, h)), srow),
        scratch_shapes=[pltpu.VMEM((2, nq, tq, tq), F32), pltpu.VMEM((nq, 2 * tq, 2 * V_DIM), BF16),
                        pltpu.VMEM((tq, 2 * V_DIM), F32),
                        pltpu.VMEM((2 * N_HEADS, 1), F32), pltpu.VMEM((2 * N_HEADS, 1), F32),
                        pltpu.VMEM((2 * N_HEADS, V_DIM), F32)])
    return pl.pallas_call(
        functools.partial(_attention_kernel, tq=tq, n_page_refs=pps, lam_init=lam_init),
        out_shape=(jax.ShapeDtypeStruct((batch, seq, ATT_W), BF16),
                   jax.ShapeDtypeStruct((nb, N_HEADS, V_DIM), F32)),
        grid_spec=grid_spec, compiler_params=_cparams(3, 52), name="attention",
    )(pt_flat, q2, k2, vh, *([ck] * pps), *([cv] * pps), q_s, kn_s, vn_s, bm, lq1, lk1, lq2, lk2, gsub)


INFO_E1, INFO_E2, INFO_W1, INFO_W2, INFO_R1, INFO_R2 = 0, 1, 2, 3, 4, 5


def _lane_min_index(mask, lane):
    return jnp.min(jnp.where(mask, lane, float(LANES)), axis=-1, keepdims=True)


def _out_proj_kernel(o_ref, yc_ref, x_ref, w_ref, gffn_ref, wr2_ref, br_ref, tri_ref, cnt_in_ref,
                     xm_ref, h2_ref, info_ref, cnt_ref, cnt_scr, *, tm):
    i = pl.program_id(0)

    @pl.when(i == 0)
    def _():
        cnt_scr[...] = cnt_in_ref[...]

    xm = x_ref[...] + _dot(o_ref[...], w_ref[0:ATT_W, :]) + _dot(yc_ref[...], w_ref[ATT_W:D_MODEL, :])
    xm_ref[...] = xm
    h2 = _rms_rows(xm, gffn_ref[...])
    h2_ref[...] = h2

    hh = h2.astype(BF16)
    hl = (h2 - hh.astype(F32)).astype(BF16)
    parts = _dot(jnp.concatenate([hh, hl], axis=0), wr2_ref[...])
    logits = (parts[0:tm, 0:LANES] + parts[0:tm, LANES:2 * LANES]
              + parts[tm:2 * tm, 0:LANES] + parts[tm:2 * tm, LANES:2 * LANES]) + br_ref[...]

    lane = lax.broadcasted_iota(jnp.int32, (tm, LANES), 1).astype(F32)
    is_grp = lane < N_GROUPS
    lg = jnp.where(is_grp, logits, NEG_BIG)
    mg = jnp.max(lg, axis=-1, keepdims=True)
    g_star = _lane_min_index(is_grp & (lg == mg), lane)
    p_group = 1.0 / jnp.sum(jnp.where(is_grp, jnp.exp(lg - mg), 0.0), axis=-1, keepdims=True)

    e_lo = N_GROUPS + g_star * EPG
    in_grp = (lane >= e_lo) & (lane < e_lo + EPG)
    le = jnp.where(in_grp, logits, NEG_BIG)
    m1 = jnp.max(le, axis=-1, keepdims=True)
    ex = jnp.where(in_grp, jnp.exp(le - m1), 0.0)
    pe = ex / jnp.sum(ex, axis=-1, keepdims=True)
    pe_m = jnp.where(in_grp, pe, -1.0)
    p1 = jnp.max(pe_m, axis=-1, keepdims=True)
    i1 = _lane_min_index(pe_m == p1, lane)
    pe_m2 = jnp.where(lane == i1, -1.0, pe_m)
    p2 = jnp.max(pe_m2, axis=-1, keepdims=True)
    i2 = _lane_min_index(pe_m2 == p2, lane)
    denom = p1 + p2
    w1 = p_group * p1 / denom
    w2 = p_group * p2 / denom
    e1 = i1 - N_GROUPS
    e2 = i2 - N_GROUPS

    oh1 = lane == e1
    oh2 = lane == e2
    oh = jnp.where(oh1 | oh2, 1.0, 0.0)
    prefix = _dot(tri_ref[...], oh.astype(BF16)) + cnt_scr[...]
    r1 = jnp.sum(jnp.where(oh1, prefix, 0.0), axis=-1, keepdims=True)
    r2 = jnp.sum(jnp.where(oh2, prefix, 0.0), axis=-1, keepdims=True)
    cnt_scr[...] = cnt_scr[...] + jnp.sum(oh, axis=0, keepdims=True)
    cnt_ref[...] = cnt_scr[...]

    info = jnp.zeros((tm, LANES), F32)
    for idx, val in ((INFO_E1, e1), (INFO_E2, e2), (INFO_W1, w1), (INFO_W2, w2),
                     (INFO_R1, r1), (INFO_R2, r2)):
        info = jnp.where(lane == idx, val, info)
    info_ref[...] = info


def _out_proj(o, yc, x, w_bf, gffn, wr2, br, tri, cnt_in, *, tm):
    t = x.shape[0]
    row = lambda i: (i, 0)
    in_specs = [
        pl.BlockSpec((tm, ATT_W), row), pl.BlockSpec((tm, CONV_W), row), pl.BlockSpec((tm, D_MODEL), row),
        _const_spec((D_MODEL, D_MODEL), single_buffer=True), _const_spec((1, D_MODEL)),
        _const_spec((D_MODEL, 2 * LANES)), _const_spec((1, LANES)),
        _const_spec((tm, tm)), _const_spec((1, LANES)),
    ]
    out_specs = (
        pl.BlockSpec((tm, D_MODEL), row),
        pl.BlockSpec((tm, D_MODEL), row),
        pl.BlockSpec((tm, LANES), row),
        _const_spec((1, LANES)),
    )
    out_shape = (
        jax.ShapeDtypeStruct((t, D_MODEL), F32), jax.ShapeDtypeStruct((t, D_MODEL), F32),
        jax.ShapeDtypeStruct((t, LANES), F32), jax.ShapeDtypeStruct((1, LANES), F32),
    )
    return pl.pallas_call(
        functools.partial(_out_proj_kernel, tm=tm),
        out_shape=out_shape, grid=(t // tm,), in_specs=in_specs, out_specs=out_specs,
        scratch_shapes=[pltpu.VMEM((1, LANES), F32)],
        compiler_params=_cparams(1, 56), name="out_proj",
    )(o, yc, x, w_bf, gffn, wr2, br, tri, cnt_in)


def _row_copy(src, dst, sem):
    return pltpu.make_async_copy(src, dst, sem)


ISSUE_UNROLL = 8


def _for_each_row(n, fn):
    def chunk(k, c):
        for u in range(ISSUE_UNROLL):
            fn(k * ISSUE_UNROLL + u)
        return c
    lax.fori_loop(0, n // ISSUE_UNROLL, chunk, 0)


def _rows(ref, first, n=1):
    return ref.at[pl.ds(first, n)]


def _dispatch_kernel(d1_ref, d2_ref, s1_ref, s2_ref, fill_ref, end_ref, na_ref,
                     h2p_ref, h2s_ref, xb_hbm, zero_scr, sem, zsem, *, tm, n_sample, n_blocks):
    step = pl.program_id(0)

    @pl.when(step == 0)
    def _():
        zero_scr[...] = jnp.zeros(zero_scr.shape, F32)

        def pad_rows(fn):
            def per_expert(e, c):
                return lax.fori_loop(fill_ref[e], end_ref[e], lambda r, c2: fn(r, c2), c)
            lax.fori_loop(0, N_EXPERTS, per_expert, 0)

        def tail_blocks(fn):
            lax.fori_loop(na_ref[0], n_blocks, lambda b, c: fn(b, c), 0)

        def start_row(r, c):
            _row_copy(_rows(zero_scr, 0), _rows(xb_hbm, r), zsem).start()
            return c

        def wait_row(r, c):
            _row_copy(_rows(zero_scr, 0), _rows(xb_hbm, 0), zsem).wait()
            return c

        def start_blk(b, c):
            _row_copy(zero_scr, _rows(xb_hbm, b * MOE_BLK, MOE_BLK), zsem).start()
            return c

        def wait_blk(b, c):
            _row_copy(zero_scr, _rows(xb_hbm, 0, MOE_BLK), zsem).wait()
            return c

        pad_rows(start_row)
        tail_blocks(start_blk)

        def sample_row(r, c):
            _row_copy(_rows(h2s_ref, r), _rows(xb_hbm, s1_ref[r]), zsem).start()
            _row_copy(_rows(h2s_ref, r), _rows(xb_hbm, s2_ref[r]), zsem).start()
            return c

        lax.fori_loop(0, n_sample, sample_row, 0)
        pad_rows(wait_row)
        tail_blocks(wait_blk)
        _row_copy(h2s_ref, _rows(xb_hbm, 0, n_sample), zsem).wait()
        _row_copy(h2s_ref, _rows(xb_hbm, 0, n_sample), zsem).wait()

    base = step * tm

    def issue(r):
        t = base + r
        _row_copy(_rows(h2p_ref, r), _rows(xb_hbm, d1_ref[t]), sem).start()
        _row_copy(_rows(h2p_ref, r), _rows(xb_hbm, d2_ref[t]), sem).start()

    _for_each_row(tm, issue)
    _row_copy(h2p_ref, _rows(xb_hbm, 0, tm), sem).wait()
    _row_copy(h2p_ref, _rows(xb_hbm, 0, tm), sem).wait()


def _dispatch(d1_p, d2_p, d1_s, d2_s, fill, end, n_active, h2_p, h2_s, *, n_blocks, tm):
    n_p = h2_p.shape[0]
    n_s = h2_s.shape[0]
    grid_spec = pltpu.PrefetchScalarGridSpec(
        num_scalar_prefetch=7, grid=(n_p // tm,),
        in_specs=[pl.BlockSpec((tm, D_MODEL), lambda i, *_: (i, 0)),
                  pl.BlockSpec((n_s, D_MODEL), lambda i, *_: (0, 0))],
        out_specs=pl.BlockSpec(memory_space=pl.ANY),
        scratch_shapes=[pltpu.VMEM((MOE_BLK, D_MODEL), F32),
                        pltpu.SemaphoreType.DMA(()), pltpu.SemaphoreType.DMA(())])
    return pl.pallas_call(
        functools.partial(_dispatch_kernel, tm=tm, n_sample=n_s, n_blocks=n_blocks),
        out_shape=jax.ShapeDtypeStruct((n_blocks * MOE_BLK, D_MODEL), F32), grid_spec=grid_spec,
        compiler_params=_cparams(1, 32), name="moe_dispatch",
    )(d1_p, d2_p, d1_s, d2_s, fill, end, n_active, h2_p, h2_s)


def _experts_kernel(be_ref, na_ref, nxt_ref, xb_ref, wg_hbm, wu_hbm, wd_hbm, yb_ref,
                    wg_f32, wu_f32, wd_f32, wg_scr, wu_scr, wd_scr, slot_ref, sems, *, layer):
    b = pl.program_id(0)
    active = b < na_ref[0]
    expert = be_ref[b]
    new_expert = (b == 0) | (expert != be_ref[jnp.maximum(b - 1, 0)])

    def weight_copies(e, slot):
        return [pltpu.make_async_copy(src.at[layer, e], dst.at[slot], sems.at[slot])
                for src, dst in ((wg_hbm, wg_f32), (wu_hbm, wu_f32), (wd_hbm, wd_f32))]

    @pl.when(b == 0)
    def _():
        slot_ref[0] = 0
        for cp in weight_copies(expert, 0):
            cp.start()

    @pl.when(active & new_expert)
    def _():
        slot = slot_ref[0]
        for cp in weight_copies(expert, slot):
            cp.wait()
        wg_scr[...] = wg_f32[slot].astype(BF16)
        wu_scr[...] = wu_f32[slot].astype(BF16)
        wd_scr[...] = wd_f32[slot].astype(BF16)
        nxt = nxt_ref[expert]

        @pl.when(nxt >= 0)
        def _():
            for cp in weight_copies(nxt, 1 - slot):
                cp.start()

        slot_ref[0] = 1 - slot

    @pl.when(active)
    def _():
        x = xb_ref[...].astype(BF16)
        g = _dot(x, wg_scr[...])
        u = _dot(x, wu_scr[...])
        a = (g * (1.0 / (1.0 + jnp.exp(-g))) * u).astype(BF16)
        yb_ref[...] = _dot(a, wd_scr[...])

    @pl.when(pl.program_id(0) >= na_ref[0])
    def _():
        yb_ref[...] = jnp.zeros(yb_ref.shape, F32)


def _experts(block_e, n_active, next_expert, xb, wg, wu, wd, *, layer):
    n_blocks = xb.shape[0] // MOE_BLK
    blk = lambda b, be, na, nx: (jnp.minimum(b, na[0] - 1), 0)
    grid_spec = pltpu.PrefetchScalarGridSpec(
        num_scalar_prefetch=3, grid=(n_blocks,),
        in_specs=[
            pl.BlockSpec((MOE_BLK, D_MODEL), blk),
            pl.BlockSpec(memory_space=pl.ANY), pl.BlockSpec(memory_space=pl.ANY), pl.BlockSpec(memory_space=pl.ANY),
        ],
        out_specs=pl.BlockSpec((MOE_BLK, D_MODEL), lambda b, be, na, nx: (b, 0)),
        scratch_shapes=[pltpu.VMEM((2, D_MODEL, D_EXPERT), F32), pltpu.VMEM((2, D_MODEL, D_EXPERT), F32),
                        pltpu.VMEM((2, D_EXPERT, D_MODEL), F32),
                        pltpu.VMEM((D_MODEL, D_EXPERT), BF16), pltpu.VMEM((D_MODEL, D_EXPERT), BF16),
                        pltpu.VMEM((D_EXPERT, D_MODEL), BF16),
                        pltpu.SMEM((1,), jnp.int32), pltpu.SemaphoreType.DMA((2,))])
    return pl.pallas_call(
        functools.partial(_experts_kernel, layer=layer),
        out_shape=jax.ShapeDtypeStruct(xb.shape, F32), grid_spec=grid_spec,
        compiler_params=_cparams(1, 52), name="moe_experts",
    )(block_e, n_active, next_expert, xb, wg, wu, wd)


def _combine_kernel(d1_ref, d2_ref, xm_ref, info_ref, yb_hbm, out_ref, a_scr, b_scr, sems, *, tm, n_tiles):
    i = pl.program_id(0)

    def fetch(tile, slot):
        base = tile * tm

        def issue(r):
            t = base + r
            _row_copy(_rows(yb_hbm, d1_ref[t]), _rows(a_scr.at[slot], r), sems.at[slot]).start()
            _row_copy(_rows(yb_hbm, d2_ref[t]), _rows(b_scr.at[slot], r), sems.at[slot]).start()

        _for_each_row(tm, issue)

    @pl.when(i == 0)
    def _():
        fetch(0, 0)

    @pl.when(i + 1 < n_tiles)
    def _():
        fetch(i + 1, (i + 1) % 2)

    slot = i % 2
    _row_copy(_rows(yb_hbm, 0, tm), a_scr.at[slot], sems.at[slot]).wait()
    _row_copy(_rows(yb_hbm, 0, tm), b_scr.at[slot], sems.at[slot]).wait()

    info = info_ref[...]
    w1 = info[:, INFO_W1:INFO_W1 + 1]
    w2 = info[:, INFO_W2:INFO_W2 + 1]
    out_ref[...] = xm_ref[...] + (a_scr[slot] * w1 + b_scr[slot] * w2)


def _combine(dest1, dest2, xm_buf, info_buf, yb, *, tm):
    n_rows = xm_buf.shape[0]
    n_tiles = n_rows // tm
    grid_spec = pltpu.PrefetchScalarGridSpec(
        num_scalar_prefetch=2, grid=(n_tiles,),
        in_specs=[
            pl.BlockSpec((tm, D_MODEL), lambda i, a, b: (i, 0)),
            pl.BlockSpec((tm, LANES), lambda i, a, b: (i, 0)),
            pl.BlockSpec(memory_space=pl.ANY),
        ],
        out_specs=pl.BlockSpec((tm, D_MODEL), lambda i, a, b: (i, 0)),
        scratch_shapes=[pltpu.VMEM((2, tm, D_MODEL), F32), pltpu.VMEM((2, tm, D_MODEL), F32),
                        pltpu.SemaphoreType.DMA((2,))])
    return pl.pallas_call(
        functools.partial(_combine_kernel, tm=tm, n_tiles=n_tiles),
        out_shape=jax.ShapeDtypeStruct((n_rows, D_MODEL), F32), grid_spec=grid_spec,
        compiler_params=_cparams(1, 48), name="moe_combine",
    )(dest1, dest2, xm_buf, info_buf, yb)


def _layer(l, xp, xs, cache_k, cache_v, state_conv, page_table, norm_mix, w_in, g_q, g_k,
           lam_q1, lam_k1, lam_q2, lam_k2, g_sub, conv_w, w_out, norm_ffn, w_rg, b_rg, w_re, b_re,
           w_gate, w_up, w_down):
    batch, seq, _ = xp.shape
    nb = xs.shape[0]
    n_pages = page_table.shape[1]
    past = n_pages * PAGE
    t_p = batch * seq
    n_tok = t_p + nb
    lam_init = 0.8 - 0.6 * math.exp(-0.3 * l)

    w_in_bf = w_in[l].astype(BF16)
    w_out_bf = w_out[l].astype(BF16)
    gmix = norm_mix[l].reshape(1, D_MODEL)
    gffn = norm_ffn[l].reshape(1, D_MODEL)
    gq = jnp.tile(g_q[l], ATT_W // QK_HALF).reshape(1, ATT_W)
    gk = jnp.tile(g_k[l], ATT_W // QK_HALF).reshape(1, ATT_W)
    gsub = g_sub[l].reshape(1, V_DIM)
    grp = jnp.arange(IN_CHUNK) // QK_HALF
    gsum = jnp.where(grp[:, None] == grp[None, :], 1.0 / QK_HALF, 0.0).astype(BF16)
    lq1, lk1, lq2, lk2 = (v[l].reshape(1, QK_HALF) for v in (lam_q1, lam_k1, lam_q2, lam_k2))
    slopes = 2.0 ** (-8.0 * jnp.arange(1, N_HEADS + 1, dtype=F32) / N_HEADS)
    lane = jnp.arange(LANES)
    sl = slopes * LOG2E
    s_hi = sl.astype(BF16).astype(F32)
    s_mid = (sl - s_hi).astype(BF16).astype(F32)
    s_lo = (sl - s_hi - s_mid).astype(BF16).astype(F32)
    terms = jnp.stack([s_hi, s_hi, s_mid, s_mid, s_lo, s_lo], axis=1)

    def query_alibi(first_lane):
        rel = lane - first_lane
        return jnp.where((rel >= 0) & (rel < ALIBI_LANES), terms[:, jnp.clip(rel, 0, ALIBI_LANES - 1)], 0.0)

    qal = jnp.stack([query_alibi(QK_HALF), query_alibi(0)], axis=1)
    w_r = jnp.zeros((D_MODEL, LANES), F32)
    w_r = w_r.at[:, 0:N_GROUPS].set(w_rg[l]).at[:, N_GROUPS:N_GROUPS + N_EXPERTS].set(w_re[l])
    wrh = w_r.astype(BF16)
    wr2 = jnp.concatenate([wrh, (w_r - wrh.astype(F32)).astype(BF16)], axis=1)
    br = jnp.zeros((1, LANES), F32)
    br = br.at[0, 0:N_GROUPS].set(b_rg[l]).at[0, N_GROUPS:N_GROUPS + N_EXPERTS].set(b_re[l])

    x2p = xp.reshape(t_p, D_MODEL)
    x2s = xs.reshape(nb, D_MODEL)
    q2, k2, vh, kf_p, vf_p, yc_p, cs_p = _in_proj_prompt(
        x2p, gmix, w_in_bf, gq, gk, gsum, conv_w[l], qal, batch=batch, seq=seq)

    prev0 = state_conv[l, :, 0, :]
    prev1 = state_conv[l, :, 1, :]
    q_s, kf_s, vf_s, yc_s, u_s = _in_proj_sample(x2s, gmix, w_in_bf, gq, gk, gsum, conv_w[l], prev0, prev1)
    row_pos = jnp.arange(past * N_HEADS, dtype=jnp.int32) // N_HEADS
    row_head = jnp.arange(past * N_HEADS, dtype=jnp.int32) % N_HEADS
    q_head = jnp.arange(2 * N_HEADS, dtype=jnp.int32) % N_HEADS
    bias = -(slopes[q_head][:, None] * (past - row_pos).astype(F32)[None, :])
    bm = jnp.where(q_head[:, None] == row_head[None, :], bias, NEG_BIG)
    bm = jnp.moveaxis(bm.reshape(2 * N_HEADS, SAMPLE_PARTS, past * N_HEADS // SAMPLE_PARTS), 1, 0)
    o_p, o_s = _attention(page_table, q2, k2, vh, cache_k, cache_v, q_s, kf_s, vf_s, bm, lq1, lk1, lq2, lk2, gsub,
                          layer=l, lam_init=lam_init)
    o_s = o_s.reshape(nb, ATT_W).astype(BF16)

    cnt0 = jnp.zeros((1, LANES), F32)
    tm_p, tm_s = 512, nb
    tri = lambda n: (jnp.arange(n)[:, None] > jnp.arange(n)[None, :]).astype(BF16)
    xm_p, h2_p, info_p, cnt1 = _out_proj(
        o_p.reshape(t_p, ATT_W), yc_p, x2p, w_out_bf, gffn, wr2, br, tri(tm_p), cnt0, tm=tm_p)
    xm_s, h2_s, info_s, cnt2 = _out_proj(
        o_s, yc_s, x2s, w_out_bf, gffn, wr2, br, tri(tm_s), cnt1, tm=tm_s)

    counts = cnt2[0, 0:N_EXPERTS].astype(jnp.int32)
    padded = (counts + MOE_BLK - 1) // MOE_BLK * MOE_BLK
    pad_end = jnp.cumsum(padded)
    pad_start = pad_end - padded
    n_blocks = -(-2 * n_tok // MOE_BLK) + N_EXPERTS
    blk_row = jnp.arange(n_blocks, dtype=jnp.int32) * MOE_BLK
    block_e = jnp.minimum(jnp.sum((pad_end[None, :] <= blk_row[:, None]).astype(jnp.int32), axis=1),
                          N_EXPERTS - 1)
    n_active = (pad_end[-1:] // MOE_BLK).astype(jnp.int32)
    experts = jnp.arange(N_EXPERTS, dtype=jnp.int32)

    def slots(info):
        def one(e_col, r_col):
            e = info[:, e_col].astype(jnp.int32)
            start = jnp.sum(jnp.where(e[:, None] == experts[None, :], pad_start[None, :], 0), axis=1)
            return start + info[:, r_col].astype(jnp.int32)
        return one(INFO_E1, INFO_R1), one(INFO_E2, INFO_R2)

    d1_p, d2_p = slots(info_p)
    d1_s, d2_s = slots(info_s)

    xb = _dispatch(d1_p, d2_p, d1_s, d2_s, pad_start + counts, pad_end, n_active, h2_p, h2_s,
                   n_blocks=n_blocks, tm=1024)
    later = (experts[None, :] > experts[:, None]) & (counts[None, :] > 0)
    nxt = jnp.min(jnp.where(later, experts[None, :], N_EXPERTS), axis=1)
    next_expert = jnp.where(nxt < N_EXPERTS, nxt, -1).astype(jnp.int32)
    yb = _experts(block_e, n_active, next_expert, xb, w_gate, w_up, w_down, layer=l)
    y_p = _combine(d1_p, d2_p, xm_p, info_p, yb, tm=512)
    y_s = _combine(d1_s, d2_s, xm_s, info_s, yb, tm=128)

    conv_s = jnp.stack([prev1, u_s], axis=1)
    return (y_p.reshape(batch, seq, D_MODEL), y_s.reshape(nb, 1, D_MODEL),
            kf_p.reshape(batch, seq, N_HEADS, V_DIM), vf_p.reshape(batch, seq, N_HEADS, V_DIM), cs_p,
            kf_s.reshape(nb, 1, N_HEADS, V_DIM), vf_s.reshape(nb, 1, N_HEADS, V_DIM), conv_s)


def kernel(x_prompt, x_sample, cache_k, cache_v, state_conv, page_table, norm_mix, w_in, g_q, g_k,
           lam_q1, lam_k1, lam_q2, lam_k2, g_sub, conv_w, w_out, norm_ffn, w_router_group, b_router_group,
           w_router_expert, b_router_expert, w_gate, w_up, w_down):
    depth = w_in.shape[0]
    xp, xs = x_prompt, x_sample
    outs = [[] for _ in range(6)]
    for l in range(depth):
        xp, xs, *rest = _layer(
            l, xp, xs, cache_k, cache_v, state_conv, page_table, norm_mix, w_in, g_q, g_k,
            lam_q1, lam_k1, lam_q2, lam_k2, g_sub, conv_w, w_out, norm_ffn, w_router_group, b_router_group,
            w_router_expert, b_router_expert, w_gate, w_up, w_down)
        for acc, val in zip(outs, rest):
            acc.append(val)
    return (xp, xs) + tuple(jnp.stack(v, axis=0) for v in outs)
```
